```python
import jax, jax.numpy as jnp
from jax import lax
import numpy as np


D_MODEL = 1024
BATCH = 16
SEQ = 2048
DEPTH = 2

N_MIXERS = 2
POOL_EXPAND = 2
D_POOL = POOL_EXPAND * D_MODEL
POOL_WINDOWS = (2, 4, 8, 16)
N_POOL_GROUPS = len(POOL_WINDOWS)
POOL_GROUP = D_POOL // N_POOL_GROUPS
HEAD_DIM = 64
N_HEADS = D_MODEL // HEAD_DIM
N_KV = 4
HPG = N_HEADS // N_KV
D_ATT = N_HEADS * HEAD_DIM
D_KV = N_KV * HEAD_DIM
N_BRANCH = 3
CMP_BLOCK = 32
CMP_STRIDE = 16
CMP_HIDDEN = 4 * HEAD_DIM
SEL_BLOCK = 64
SEL_TOPN = 8
WINDOW = 256
Q_BLOCK = 64
NSA_SPLITS = (D_ATT, D_KV, D_KV, D_KV, D_KV, D_KV, D_KV, D_ATT, N_BRANCH * N_HEADS)
NSA_VALUE_SLOTS = (2, 4, 6)
D_NSA_IN = sum(NSA_SPLITS)

N_POOL_LAYERS = (DEPTH + N_MIXERS - 1) // N_MIXERS
N_NSA_LAYERS = DEPTH // N_MIXERS

DN_ALPHA = (2.0 * DEPTH) ** 0.25
DN_BETA = (8.0 * DEPTH) ** -0.25
LN_EPS = 1e-5
NEG_INF = -1e30
FORCE_SCORE = 1e9

kernel_name = 'hybrid_pool_nsa_deepnorm'


def layer_norm(x, g, b):
    xf = x.astype(jnp.float32)
    mu = jnp.mean(xf, axis=-1, keepdims=True)
    var = jnp.mean(jnp.square(xf - mu), axis=-1, keepdims=True)
    y = (xf - mu) * lax.rsqrt(var + LN_EPS) * g.astype(jnp.float32) + b.astype(jnp.float32)
    return y.astype(x.dtype)


def alibi_slopes():
    h = jnp.arange(1, N_HEADS + 1, dtype=jnp.float32)
    return (2.0 ** (-8.0 * h / N_HEADS)).reshape(N_KV, HPG)


def masked_softmax(s, valid):
    p = jax.nn.softmax(jnp.where(valid, s, NEG_INF), axis=-1)
    return jnp.where(valid, p, 0.0)


def pool_mixer(x, w_in, w_grp, scale, w_out):
    B, S, _ = x.shape
    h = x @ w_in
    u, z = h[..., :D_POOL], h[..., D_POOL:]
    u = u.astype(jnp.float32).reshape(B, S, N_POOL_GROUPS, POOL_GROUP)
    csum = jnp.cumsum(u, axis=1)
    t = jnp.arange(S)
    outs = []
    for g, w in enumerate(POOL_WINDOWS):
        c = csum[:, :, g]
        lag = jnp.pad(c, ((0, 0), (w, 0), (0, 0)))[:, :S]
        cnt = jnp.minimum(t + 1, w).astype(jnp.float32)[None, :, None]
        outs.append((c - lag) / cnt - u[:, :, g])
    m = jnp.stack(outs, axis=2).astype(x.dtype)
    m = jnp.einsum('bsgc,gcd->bsgd', m, w_grp).reshape(B, S, D_POOL) * scale
    return (m * jax.nn.silu(z)) @ w_out


def nsa_mixer(x, w_in, pos_k, w1_k, w2_k, pos_v, w1_v, w2_v, w_out):
    B, S, _ = x.shape
    dt = x.dtype
    h = x @ w_in
    q, kc, vc, ks, vs, kw, vw, z, gl = jnp.split(h, np.cumsum(NSA_SPLITS)[:-1].tolist(), axis=-1)
    q = q.reshape(B, S, N_KV, HPG, HEAD_DIM) * (HEAD_DIM ** -0.5)
    kc, vc, ks, vs, kw, vw = [a.reshape(B, S, N_KV, HEAD_DIM) for a in (kc, vc, ks, vs, kw, vw)]
    gates = jax.nn.sigmoid(gl.astype(jnp.float32)).reshape(B, S, N_KV, HPG, N_BRANCH)
    slopes = alibi_slopes()[None, :, :, None, None]

    n_cmp = (S - CMP_BLOCK) // CMP_STRIDE + 1
    cmp_idx = np.arange(n_cmp)[:, None] * CMP_STRIDE + np.arange(CMP_BLOCK)[None, :]
    cmp_end = jnp.asarray(cmp_idx[:, -1], dtype=jnp.int32)

    def compress(a, pos, w1, w2):
        blk = a[:, cmp_idx] + pos[None, None, :, None, :]
        blk = blk.transpose(0, 1, 3, 2, 4).reshape(B, n_cmp, N_KV, CMP_BLOCK * HEAD_DIM)
        return jax.nn.silu(blk @ w1) @ w2

    k_cmp = compress(kc, pos_k, w1_k, w2_k)
    v_cmp = compress(vc, pos_v, w1_v, w2_v)

    n_sel = S // SEL_BLOCK
    top_n = min(SEL_TOPN, n_sel)
    c0 = np.arange(n_cmp)[:, None] * CMP_STRIDE
    j0 = np.arange(n_sel)[None, :] * SEL_BLOCK
    overlap = jnp.asarray((c0 < j0 + SEL_BLOCK) & (c0 + CMP_BLOCK > j0), dtype=jnp.float32)
    ks_blk = ks.reshape(B, n_sel, SEL_BLOCK, N_KV, HEAD_DIM).transpose(0, 3, 1, 2, 4)
    vs_blk = vs.reshape(B, n_sel, SEL_BLOCK, N_KV, HEAD_DIM).transpose(0, 3, 1, 2, 4)
    b_ix = jnp.arange(B)[:, None, None, None]
    g_ix = jnp.arange(N_KV)[None, :, None, None]
    blk_ids = jnp.arange(n_sel)
    in_blk = jnp.arange(SEL_BLOCK)

    kw_pad = jnp.pad(kw, ((0, 0), (WINDOW, 0), (0, 0), (0, 0)))
    vw_pad = jnp.pad(vw, ((0, 0), (WINDOW, 0), (0, 0), (0, 0)))
    win_off = jnp.arange(WINDOW + Q_BLOCK) - WINDOW

    def block(i):
        q0 = i * Q_BLOCK
        qb = lax.dynamic_slice_in_dim(q, q0, Q_BLOCK, axis=1)
        gb = lax.dynamic_slice_in_dim(gates, q0, Q_BLOCK, axis=1)
        t = q0 + jnp.arange(Q_BLOCK)

        dist = t[:, None] - cmp_end[None, :]
        s = jnp.einsum('bqghd,bcgd->bghqc', qb, k_cmp).astype(jnp.float32)
        s = s - slopes * dist.astype(jnp.float32)
        p_cmp = masked_softmax(s, dist >= 0)
        o_cmp = jnp.einsum('bghqc,bcgd->bqghd', p_cmp.astype(dt), v_cmp)

        imp = jnp.einsum('bghqc,cn->bgqn', p_cmp, overlap)
        cur = t // SEL_BLOCK
        forced = (blk_ids[None] == 0) | (blk_ids[None] == cur[:, None]) | (blk_ids[None] == cur[:, None] - 1)
        future = blk_ids[None] > cur[:, None]
        imp = jnp.where(forced, FORCE_SCORE, jnp.where(future, NEG_INF, imp))
        _, sel = lax.top_k(imp, top_n)
        k_g = ks_blk[b_ix, g_ix, sel].reshape(B, N_KV, Q_BLOCK, top_n * SEL_BLOCK, HEAD_DIM)
        v_g = vs_blk[b_ix, g_ix, sel].reshape(B, N_KV, Q_BLOCK, top_n * SEL_BLOCK, HEAD_DIM)
        pos = (sel[..., None] * SEL_BLOCK + in_blk).reshape(B, N_KV, Q_BLOCK, top_n * SEL_BLOCK)
        dist = t[None, None, :, None] - pos
        s = jnp.einsum('bqghd,bgqkd->bghqk', qb, k_g).astype(jnp.float32)
        s = s - slopes * dist[:, :, None].astype(jnp.float32)
        p = masked_softmax(s, (dist >= 0)[:, :, None])
        o_sel = jnp.einsum('bghqk,bgqkd->bqghd', p.astype(dt), v_g)

        kwb = lax.dynamic_slice_in_dim(kw_pad, q0, WINDOW + Q_BLOCK, axis=1)
        vwb = lax.dynamic_slice_in_dim(vw_pad, q0, WINDOW + Q_BLOCK, axis=1)
        spos = q0 + win_off
        dist = t[:, None] - spos[None, :]
        valid = (dist >= 0) & (dist < WINDOW) & (spos[None, :] >= 0)
        s = jnp.einsum('bqghd,bkgd->bghqk', qb, kwb).astype(jnp.float32)
        s = s - slopes * dist.astype(jnp.float32)
        p = masked_softmax(s, valid)
        o_win = jnp.einsum('bghqk,bkgd->bqghd', p.astype(dt), vwb)

        o = gb[..., 0:1] * o_cmp + gb[..., 1:2] * o_sel + gb[..., 2:3] * o_win
        return o.astype(dt).reshape(B, Q_BLOCK, D_ATT)

    o = lax.map(block, jnp.arange(S // Q_BLOCK))
    o = o.transpose(1, 0, 2, 3).reshape(B, S, D_ATT)
    return (o * jax.nn.silu(z)) @ w_out


def setup_inputs(seed: int = 0) -> dict:
    key = jax.random.key(seed)
    k = jax.random.split(key, 16)
    nA, nB = N_POOL_LAYERS, N_NSA_LAYERS

    def nrm(kk, shape, scale):
        return jax.random.normal(kk, shape, jnp.float32) * scale

    pool_col = jnp.asarray(np.concatenate([np.full(D_POOL, DN_BETA), np.ones(D_POOL)]), dtype=jnp.float32)
    nsa_col = jnp.asarray(np.concatenate([np.full(n, DN_BETA if s in NSA_VALUE_SLOTS else 1.0)
                                          for s, n in enumerate(NSA_SPLITS)]), dtype=jnp.float32)
    return {
        'x': nrm(k[0], (BATCH, SEQ, D_MODEL), 1.0),
        'ln_g': 1.0 + nrm(k[1], (DEPTH, D_MODEL), 0.02),
        'ln_b': nrm(k[2], (DEPTH, D_MODEL), 0.02),
        'pool_w_in': nrm(k[3], (nA, D_MODEL, 2 * D_POOL), D_MODEL ** -0.5) * pool_col,
        'pool_w_grp': nrm(k[4], (nA, N_POOL_GROUPS, POOL_GROUP, POOL_GROUP), POOL_GROUP ** -0.5),
        'pool_scale': 1.0 + nrm(k[5], (nA, D_POOL), 0.02),
        'pool_w_out': nrm(k[6], (nA, D_POOL, D_MODEL), D_POOL ** -0.5 * DN_BETA),
        'nsa_w_in': nrm(k[7], (nB, D_MODEL, D_NSA_IN), D_MODEL ** -0.5) * nsa_col,
        'nsa_cmp_pos_k': nrm(k[8], (nB, CMP_BLOCK, HEAD_DIM), 0.02),
        'nsa_cmp_w1_k': nrm(k[9], (nB, CMP_BLOCK * HEAD_DIM, CMP_HIDDEN), (CMP_BLOCK * HEAD_DIM) ** -0.5),
        'nsa_cmp_w2_k': nrm(k[10], (nB, CMP_HIDDEN, HEAD_DIM), CMP_HIDDEN ** -0.5),
        'nsa_cmp_pos_v': nrm(k[11], (nB, CMP_BLOCK, HEAD_DIM), 0.02),
        'nsa_cmp_w1_v': nrm(k[12], (nB, CMP_BLOCK * HEAD_DIM, CMP_HIDDEN), (CMP_BLOCK * HEAD_DIM) ** -0.5),
        'nsa_cmp_w2_v': nrm(k[13], (nB, CMP_HIDDEN, HEAD_DIM), CMP_HIDDEN ** -0.5),
        'nsa_w_out': nrm(k[14], (nB, D_ATT, D_MODEL), D_ATT ** -0.5 * DN_BETA),
    }


def reference(x, ln_g, ln_b, pool_w_in, pool_w_grp, pool_scale, pool_w_out,
              nsa_w_in, nsa_cmp_pos_k, nsa_cmp_w1_k, nsa_cmp_w2_k,
              nsa_cmp_pos_v, nsa_cmp_w1_v, nsa_cmp_w2_v, nsa_w_out):
    for i in range(DEPTH):
        j = i // N_MIXERS
        if i % N_MIXERS == 0:
            y = pool_mixer(x, pool_w_in[j], pool_w_grp[j], pool_scale[j], pool_w_out[j])
        else:
            y = nsa_mixer(x, nsa_w_in[j], nsa_cmp_pos_k[j], nsa_cmp_w1_k[j], nsa_cmp_w2_k[j],
                          nsa_cmp_pos_v[j], nsa_cmp_w1_v[j], nsa_cmp_w2_v[j], nsa_w_out[j])
        x = layer_norm(DN_ALPHA * x + y, ln_g[i], ln_b[i])
    return x
```

```python
import functools

import jax
import jax.numpy as jnp
from jax import lax
from jax.experimental import pallas as pl
from jax.experimental.pallas import tpu as pltpu

F32 = jnp.float32
BF16 = jnp.bfloat16

D_MODEL = 1024
DEPTH = 2
D_POOL = 2 * D_MODEL
POOL_WINDOWS = (2, 4, 8, 16)
POOL_GROUP = D_POOL // len(POOL_WINDOWS)
HEAD_DIM = 64
N_HEADS = D_MODEL // HEAD_DIM
N_KV = 4
HPG = N_HEADS // N_KV
D_ATT = N_HEADS * HEAD_DIM
D_KV = N_KV * HEAD_DIM
N_BRANCH = 3
CMP_BLOCK = 32
CMP_STRIDE = 16
SEL_BLOCK = 64
SEL_TOPN = 8
WINDOW = 256
DN_ALPHA = (2.0 * DEPTH) ** 0.25
LN_EPS = 1e-5
NEG_INF = -1e30
FORCE_SCORE = 1e9

LANES = 128
SUBLANES = 8
VMEM_LIMIT_BYTES = 56 * 1024 * 1024

POOL_TM = 512
POOL_HALO = 16
PROJ_TM = 512
ATT_TQ = 256
ATT_KB = 512
GATE_LANES = LANES


def _dot(a, b):
    return jnp.dot(a, b, preferred_element_type=F32)


def _dot_nt(a, b):
    return lax.dot_general(a, b, (((1,), (1,)), ((), ())), preferred_element_type=F32)


def _div_pow2(x, n):
    assert n & (n - 1) == 0
    return lax.shift_right_logical(x, n.bit_length() - 1)


def _silu(x):
    return x * jax.nn.sigmoid(x)


def _layer_norm(r, g, b):
    mu = jnp.mean(r, axis=-1, keepdims=True)
    d = r - mu
    var = jnp.mean(d * d, axis=-1, keepdims=True)
    return d * lax.rsqrt(var + LN_EPS) * g + b


def _const_spec(shape):
    n = len(shape)
    return pl.BlockSpec(shape, lambda *_: (0,) * n, pipeline_mode=pl.Buffered(1))


def _pool_layer_kernel(x_ref, w_in_ref, w_grp_ref, scale_ref, w_out_ref, g_ref, b_ref,
                       o_ref, carry_ref, work_ref):
    tm = x_ref.shape[1]
    s = pl.program_id(1)

    @pl.when(s == 0)
    def _():
        carry_ref[...] = jnp.zeros_like(carry_ref)

    xt = x_ref[0]
    xb = xt.astype(BF16)
    t = s * tm + lax.broadcasted_iota(jnp.int32, (tm, 1), 0)
    y = jnp.zeros((tm, D_MODEL), F32)
    for g, w in enumerate(POOL_WINDOWS):
        lo = g * POOL_GROUP
        u = _dot(xb, w_in_ref[:, lo:lo + POOL_GROUP])
        z = _dot(xb, w_in_ref[:, D_POOL + lo:D_POOL + lo + POOL_GROUP])
        work_ref[0:POOL_HALO, :] = carry_ref[g]
        work_ref[POOL_HALO:, :] = u
        carry_ref[g] = u[tm - POOL_HALO:, :]
        tot = u
        for i in range(1, w):
            tot = tot + work_ref[POOL_HALO - i:POOL_HALO - i + tm, :]
        inv_cnt = 1.0 / jnp.minimum(t + 1, w).astype(F32)
        pooled = tot * inv_cnt - u
        m = _dot(pooled.astype(BF16), w_grp_ref[g]) * scale_ref[:, lo:lo + POOL_GROUP]
        gated = m * _silu(z)
        y = y + _dot(gated.astype(BF16), w_out_ref[lo:lo + POOL_GROUP, :])
    o_ref[0] = _layer_norm(DN_ALPHA * xt + y, g_ref[...], b_ref[...])


def _pool_layer(x, w_in, w_grp, scale, w_out, ln_g, ln_b):
    B, S, D = x.shape
    tm = POOL_TM
    return pl.pallas_call(
        _pool_layer_kernel,
        grid=(B, S // tm),
        in_specs=[
            pl.BlockSpec((1, tm, D), lambda b, s: (b, s, 0)),
            _const_spec(w_in.shape),
            _const_spec(w_grp.shape),
            _const_spec(scale.shape),
            _const_spec(w_out.shape),
            _const_spec(ln_g.shape),
            _const_spec(ln_b.shape),
        ],
        out_specs=pl.BlockSpec((1, tm, D), lambda b, s: (b, s, 0)),
        out_shape=jax.ShapeDtypeStruct((B, S, D), F32),
        scratch_shapes=[
            pltpu.VMEM((len(POOL_WINDOWS), POOL_HALO, POOL_GROUP), F32),
            pltpu.VMEM((POOL_HALO + tm, POOL_GROUP), F32),
        ],
        compiler_params=pltpu.CompilerParams(
            dimension_semantics=("parallel", "arbitrary"),
            vmem_limit_bytes=VMEM_LIMIT_BYTES),
        name="pool_layer",
    )(x, w_in, w_grp, scale, w_out, ln_g, ln_b)


def _nsa_inproj_kernel(x_ref, wq_ref, wkv_ref, wz_ref, wgl_ref,
                       q_ref, cmp_ref, kvs_ref, z_ref, gate_ref):
    xb = x_ref[0].astype(BF16)
    q_ref[0] = (_dot(xb, wq_ref[...]) * (HEAD_DIM ** -0.5)).astype(BF16)
    kv = _dot(xb, wkv_ref[...])
    for a in range(6):
        for g in range(N_KV):
            lo = a * D_KV + g * HEAD_DIM
            piece = kv[:, lo:lo + HEAD_DIM]
            if a < 2:
                cmp_ref[0, a, g] = piece
            else:
                kvs_ref[0, a - 2, g] = piece.astype(BF16)
    z_ref[0] = _dot(xb, wz_ref[...])
    gates = jax.nn.sigmoid(_dot(xb, wgl_ref[...]))
    per_group = HPG * N_BRANCH
    for g in range(N_KV):
        gate_ref[0, g] = gates[:, g * per_group:(g + 1) * per_group]


def _nsa_inproj(x, wq, wkv, wz, wgl):
    B, S, D = x.shape
    tm = PROJ_TM
    per_group = HPG * N_BRANCH
    return pl.pallas_call(
        _nsa_inproj_kernel,
        grid=(B, S // tm),
        in_specs=[
            pl.BlockSpec((1, tm, D), lambda b, s: (b, s, 0)),
            _const_spec(wq.shape),
            _const_spec(wkv.shape),
            _const_spec(wz.shape),
            _const_spec(wgl.shape),
        ],
        out_specs=[
            pl.BlockSpec((1, tm, D_ATT), lambda b, s: (b, s, 0)),
            pl.BlockSpec((1, 2, N_KV, tm, HEAD_DIM), lambda b, s: (b, 0, 0, s, 0)),
            pl.BlockSpec((1, 4, N_KV, tm, HEAD_DIM), lambda b, s: (b, 0, 0, s, 0)),
            pl.BlockSpec((1, tm, D_ATT), lambda b, s: (b, s, 0)),
            pl.BlockSpec((1, N_KV, tm, per_group), lambda b, s: (b, 0, s, 0)),
        ],
        out_shape=[
            jax.ShapeDtypeStruct((B, S, D_ATT), BF16),
            jax.ShapeDtypeStruct((B, 2, N_KV, S, HEAD_DIM), F32),
            jax.ShapeDtypeStruct((B, 4, N_KV, S, HEAD_DIM), BF16),
            jax.ShapeDtypeStruct((B, S, D_ATT), F32),
            jax.ShapeDtypeStruct((B, N_KV, S, per_group), F32),
        ],
        compiler_params=pltpu.CompilerParams(
            dimension_semantics=("parallel", "parallel"),
            vmem_limit_bytes=VMEM_LIMIT_BYTES),
        name="nsa_inproj",
    )(x, wq, wkv, wz, wgl)


def _compress_kernel(c_ref, pos_ref, w1_ref, w2_ref, o_ref):
    n_grp, n_row, width = c_ref.shape[2:]
    rows = c_ref[0, 0].reshape(n_grp * n_row, width)
    nxt = pltpu.roll(rows, shift=n_grp * n_row - 1, axis=0)
    first = (rows + pos_ref[0, 0:1, :]).astype(BF16)
    second = (nxt + pos_ref[0, 1:2, :]).astype(BF16)
    h = _dot(first, w1_ref[0, :width, :]) + _dot(second, w1_ref[0, width:, :])
    out = _dot(_silu(h).astype(BF16), w2_ref[0])
    o_ref[0, 0] = out.reshape(n_grp, n_row, HEAD_DIM).astype(BF16)


def _compress(cmp_in, pos, w1, w2):
    B = cmp_in.shape[0]
    S = cmp_in.shape[3]
    n_row = S // CMP_STRIDE
    width = CMP_STRIDE * HEAD_DIM
    c = cmp_in.reshape(B, 2, N_KV, n_row, width)
    return pl.pallas_call(
        _compress_kernel,
        grid=(2, B),
        in_specs=[
            pl.BlockSpec((1, 1, N_KV, n_row, width), lambda a, b: (b, a, 0, 0, 0)),
            pl.BlockSpec((1, 2, width), lambda a, b: (a, 0, 0)),
            pl.BlockSpec((1,) + w1.shape[1:], lambda a, b: (a, 0, 0)),
            pl.BlockSpec((1,) + w2.shape[1:], lambda a, b: (a, 0, 0)),
        ],
        out_specs=pl.BlockSpec((1, 1, N_KV, n_row, HEAD_DIM), lambda a, b: (b, a, 0, 0, 0)),
        out_shape=jax.ShapeDtypeStruct((B, 2, N_KV, n_row, HEAD_DIM), BF16),
        compiler_params=pltpu.CompilerParams(
            dimension_semantics=("parallel", "parallel"),
            vmem_limit_bytes=VMEM_LIMIT_BYTES),
        name="nsa_compress",
    )(c, pos, w1, w2)


def _attn_kernel(slope_ref, q_ref, cmp_ref, kvs_ref, gate_ref, o_ref):
    tq = q_ref.shape[1]
    n_cmp_pad = cmp_ref.shape[3]
    seq = kvs_ref.shape[3]
    n_sel = seq // SEL_BLOCK
    kb = ATT_KB
    grp = pl.program_id(1)
    q0 = pl.program_id(2) * tq

    qb = q_ref[0]
    qh = [qb[:, h * HEAD_DIM:(h + 1) * HEAD_DIM] for h in range(HPG)]
    slopes = [slope_ref[grp * HPG + h] for h in range(HPG)]
    t_col = q0 + lax.broadcasted_iota(jnp.int32, (tq, 1), 0)

    k_cmp = cmp_ref[0, 0, 0]
    v_cmp = cmp_ref[0, 1, 0]
    c_row = lax.broadcasted_iota(jnp.int32, (1, n_cmp_pad), 1)
    dist_c = t_col - (c_row * CMP_STRIDE + (CMP_BLOCK - 1))
    valid_c = dist_c >= 0
    dist_cf = dist_c.astype(F32)
    p_sum = jnp.zeros((tq, n_cmp_pad), F32)
    o_cmp = []
    for h in range(HPG):
        s = _dot_nt(qh[h], k_cmp) - slopes[h] * dist_cf
        s = jnp.where(valid_c, s, NEG_INF)
        m = jnp.max(s, axis=-1, keepdims=True)
        p = jnp.where(valid_c, jnp.exp(s - m), 0.0)
        l = jnp.sum(p, axis=-1, keepdims=True)
        p = p * jnp.where(l > 0.0, 1.0 / l, 0.0)
        p_sum = p_sum + p
        o_cmp.append(_dot(p.astype(BF16), v_cmp))

    j_ov = lax.broadcasted_iota(jnp.int32, (n_sel, n_cmp_pad), 0) * SEL_BLOCK
    c_ov = lax.broadcasted_iota(jnp.int32, (n_sel, n_cmp_pad), 1) * CMP_STRIDE
    overlap_t = jnp.where((c_ov < j_ov + SEL_BLOCK) & (c_ov + CMP_BLOCK > j_ov), 1.0, 0.0).astype(BF16)
    p_hi = p_sum.astype(BF16)
    p_lo = (p_sum - p_hi.astype(F32)).astype(BF16)
    imp = _dot_nt(overlap_t, p_hi) + _dot_nt(overlap_t, p_lo)
    j_blk = lax.broadcasted_iota(jnp.int32, (n_sel, tq), 0)
    cur = _div_pow2(q0 + lax.broadcasted_iota(jnp.int32, (n_sel, tq), 1), SEL_BLOCK)
    forced = (j_blk == 0) | (j_blk == cur) | (j_blk == cur - 1)
    imp = jnp.where(forced, FORCE_SCORE, jnp.where(j_blk > cur, NEG_INF, imp))
    rank = jnp.zeros((n_sel, tq), jnp.int32)
    for i in range(n_sel):
        row = imp[i:i + 1, :]
        rank = rank + jnp.where(j_blk > i, jnp.where(row >= imp, 1, 0), jnp.where(row > imp, 1, 0))
    sel_t = jnp.where(rank < min(SEL_TOPN, n_sel), 1.0, 0.0)
    sel = jnp.transpose(sel_t).astype(BF16)

    def sel_step(c, carry):
        k0 = pl.multiple_of(c * kb, kb)
        k_c = kvs_ref[0, 0, 0, pl.ds(k0, kb), :]
        v_c = kvs_ref[0, 1, 0, pl.ds(k0, kb), :]
        pos = k0 + lax.broadcasted_iota(jnp.int32, (1, kb), 1)
        blk_of_key = _div_pow2(k0 + lax.broadcasted_iota(jnp.int32, (n_sel, kb), 1), SEL_BLOCK)
        expand = jnp.where(blk_of_key == lax.broadcasted_iota(jnp.int32, (n_sel, kb), 0), 1.0, 0.0).astype(BF16)
        valid = (_dot(sel, expand) > 0.5) & (pos <= t_col)
        pos_rel = (pos - q0).astype(F32)
        out = []
        for h in range(HPG):
            m_old, l_old, acc_old = carry[h]
            s = _dot_nt(qh[h], k_c) + slopes[h] * pos_rel
            s = jnp.where(valid, s, NEG_INF)
            m_new = jnp.maximum(m_old, jnp.max(s, axis=-1, keepdims=True))
            scale = jnp.exp(m_old - m_new)
            p = jnp.exp(s - m_new)
            l_new = scale * l_old + jnp.sum(p, axis=-1, keepdims=True)
            acc_new = scale * acc_old + _dot(p.astype(BF16), v_c)
            out.append((m_new, l_new, acc_new))
        return tuple(out)

    init = tuple((jnp.full((tq, 1), NEG_INF, F32), jnp.zeros((tq, 1), F32), jnp.zeros((tq, HEAD_DIM), F32))
                 for _ in range(HPG))
    n_chunks = _div_pow2(q0 + tq + kb - 1, kb)
    sel_state = lax.fori_loop(0, n_chunks, sel_step, init)
    o_sel = [acc / l for (_, l, acc) in sel_state]

    n_win = WINDOW + tq
    w0 = pl.multiple_of(jnp.maximum(q0 - WINDOW, 0), LANES)
    k_w = kvs_ref[0, 2, 0, pl.ds(w0, n_win), :]
    v_w = kvs_ref[0, 3, 0, pl.ds(w0, n_win), :]
    pos_w = w0 + lax.broadcasted_iota(jnp.int32, (1, n_win), 1)
    dist_w = t_col - pos_w
    valid_w = (dist_w >= 0) & (dist_w < WINDOW)
    pos_rel_w = (pos_w - q0).astype(F32)
    gates = gate_ref[0, 0]
    outs = []
    for h in range(HPG):
        s = _dot_nt(qh[h], k_w) + slopes[h] * pos_rel_w
        s = jnp.where(valid_w, s, NEG_INF)
        m = jnp.max(s, axis=-1, keepdims=True)
        p = jnp.exp(s - m)
        l = jnp.sum(p, axis=-1, keepdims=True)
        o_win = _dot(p.astype(BF16), v_w) / l
        gb = gates[:, h * N_BRANCH:(h + 1) * N_BRANCH]
        outs.append(gb[:, 0:1] * o_cmp[h] + gb[:, 1:2] * o_sel[h] + gb[:, 2:3] * o_win)
    o_ref[0] = jnp.concatenate(outs, axis=-1)


def _attention(slopes, q, cmp_kv, kvs, gates):
    B, S, _ = q.shape
    tq = ATT_TQ
    n_cmp_pad = cmp_kv.shape[3]
    per_group = HPG * N_BRANCH
    grid_spec = pltpu.PrefetchScalarGridSpec(
        num_scalar_prefetch=1,
        grid=(B, N_KV, S // tq),
        in_specs=[
            pl.BlockSpec((1, tq, HPG * HEAD_DIM), lambda b, g, i, sl: (b, i, g)),
            pl.BlockSpec((1, 2, 1, n_cmp_pad, HEAD_DIM), lambda b, g, i, sl: (b, 0, g, 0, 0)),
            pl.BlockSpec((1, 4, 1, S, HEAD_DIM), lambda b, g, i, sl: (b, 0, g, 0, 0)),
            pl.BlockSpec((1, 1, tq, per_group), lambda b, g, i, sl: (b, g, i, 0)),
        ],
        out_specs=pl.BlockSpec((1, tq, HPG * HEAD_DIM), lambda b, g, i, sl: (b, i, g)),
    )
    return pl.pallas_call(
        _attn_kernel,
        grid_spec=grid_spec,
        out_shape=jax.ShapeDtypeStruct((B, S, D_ATT), F32),
        compiler_params=pltpu.CompilerParams(
            dimension_semantics=("parallel", "parallel", "arbitrary"),
            vmem_limit_bytes=VMEM_LIMIT_BYTES),
        name="nsa_attention",
    )(slopes, q, cmp_kv, kvs, gates)


def _nsa_out_kernel(o_ref, z_ref, x_ref, w_ref, g_ref, b_ref, out_ref):
    gated = o_ref[0] * _silu(z_ref[0])
    y = _dot(gated.astype(BF16), w_ref[...])
    out_ref[0] = _layer_norm(DN_ALPHA * x_ref[0] + y, g_ref[...], b_ref[...])


def _nsa_out(o, z, x, w_out, ln_g, ln_b):
    B, S, D = x.shape
    tm = PROJ_TM
    tile = pl.BlockSpec((1, tm, D), lambda b, s: (b, s, 0))
    return pl.pallas_call(
        _nsa_out_kernel,
        grid=(B, S // tm),
        in_specs=[tile, tile, tile, _const_spec(w_out.shape), _const_spec(ln_g.shape), _const_spec(ln_b.shape)],
        out_specs=tile,
        out_shape=jax.ShapeDtypeStruct((B, S, D), F32),
        compiler_params=pltpu.CompilerParams(
            dimension_semantics=("parallel", "parallel"),
            vmem_limit_bytes=VMEM_LIMIT_BYTES),
        name="nsa_out",
    )(o, z, x, w_out, ln_g, ln_b)


def _alibi_slopes():
    h = jnp.arange(1, N_HEADS + 1, dtype=F32)
    return 2.0 ** (-8.0 * h / N_HEADS)


def kernel(x, ln_g, ln_b, pool_w_in, pool_w_grp, pool_scale, pool_w_out, nsa_w_in, nsa_cmp_pos_k, nsa_cmp_w1_k,
           nsa_cmp_w2_k, nsa_cmp_pos_v, nsa_cmp_w1_v, nsa_cmp_w2_v, nsa_w_out):
    assert x.shape[1] % max(POOL_TM, PROJ_TM, ATT_TQ, ATT_KB) == 0 and x.shape[2] == D_MODEL
    assert x.shape[1] >= WINDOW + ATT_TQ and WINDOW % ATT_TQ == 0

    x = _pool_layer(x, pool_w_in[0].astype(BF16), pool_w_grp[0].astype(BF16), pool_scale[0][None, :],
                    pool_w_out[0].astype(BF16), ln_g[0][None, :], ln_b[0][None, :])

    w = nsa_w_in[0]
    kv_lo, z_lo, gl_lo = D_ATT, D_ATT + 6 * D_KV, 2 * D_ATT + 6 * D_KV
    wq = w[:, :kv_lo].astype(BF16)
    wkv = w[:, kv_lo:z_lo].astype(BF16)
    wz = w[:, z_lo:gl_lo].astype(BF16)
    wgl = jnp.pad(w[:, gl_lo:], ((0, 0), (0, GATE_LANES - N_BRANCH * N_HEADS))).astype(BF16)
    q, cmp_in, kvs, z, gates = _nsa_inproj(x, wq, wkv, wz, wgl)

    half = CMP_BLOCK // 2
    pos = jnp.stack([nsa_cmp_pos_k[0], nsa_cmp_pos_v[0]]).reshape(2, 2, half * HEAD_DIM)
    w1 = jnp.stack([nsa_cmp_w1_k[0], nsa_cmp_w1_v[0]]).astype(BF16)
    w2 = jnp.stack([nsa_cmp_w2_k[0], nsa_cmp_w2_v[0]]).astype(BF16)
    cmp_kv = _compress(cmp_in, pos, w1, w2)

    o = _attention(_alibi_slopes(), q, cmp_kv, kvs, gates)
    return _nsa_out(o, z, x, nsa_w_out[0].astype(BF16), ln_g[1][None, :], ln_b[1][None, :])
```

```python
import math

import jax
import jax.numpy as jnp
from jax import lax
from jax.experimental import pallas as pl
from jax.experimental.pallas import tpu as pltpu

F32 = jnp.float32
BF16 = jnp.bfloat16

D_MODEL = 1024
DEPTH = 2
D_POOL = 2 * D_MODEL
POOL_WINDOWS = (2, 4, 8, 16)
POOL_GROUP = D_POOL // len(POOL_WINDOWS)
HEAD_DIM = 64
N_HEADS = D_MODEL // HEAD_DIM
N_KV = 4
HPG = N_HEADS // N_KV
D_ATT = N_HEADS * HEAD_DIM
D_KV = N_KV * HEAD_DIM
N_BRANCH = 3
CMP_BLOCK = 32
CMP_STRIDE = 16
SEL_BLOCK = 64
SEL_TOPN = 8
WINDOW = 256
DN_ALPHA = (2.0 * DEPTH) ** 0.25
LN_EPS = 1e-5
NEG_INF = -1e30
FORCE_SCORE = 1e9
LOG2E = math.log2(math.e)

LANES = 128
SUBLANES = 8
VMEM_LIMIT_BYTES = 56 * 1024 * 1024

POOL_TM = 512
POOL_HALO = 16
PROJ_TM = 512
ATT_TQ = 256
ATT_KB = 512
GATE_LANES = LANES

N_SLOPE_PIECES = 3
EXT_ONEHOT = 32
SLOPE_ROW = 8
EXT_LANES = LANES - HEAD_DIM


def _dot(a, b):
    return jnp.dot(a, b, preferred_element_type=F32)


def _dot_nt(a, b):
    return lax.dot_general(a, b, (((1,), (1,)), ((), ())), preferred_element_type=F32)


def _div_pow2(x, n):
    assert n & (n - 1) == 0
    return lax.shift_right_logical(x, n.bit_length() - 1)


def _silu(x):
    return x * jax.nn.sigmoid(x)


def _layer_norm(r, g, b):
    mu = jnp.mean(r, axis=-1, keepdims=True)
    d = r - mu
    var = jnp.mean(d * d, axis=-1, keepdims=True)
    return d * lax.rsqrt(var + LN_EPS) * g + b


def _const_spec(shape):
    n = len(shape)
    return pl.BlockSpec(shape, lambda *_: (0,) * n, pipeline_mode=pl.Buffered(1))


def _pool_layer_kernel(x_ref, w_in_ref, w_grp_ref, scale_ref, w_out_ref, g_ref, b_ref,
                       o_ref, carry_ref, work_ref):
    tm = x_ref.shape[1]
    s = pl.program_id(1)

    @pl.when(s == 0)
    def _():
        carry_ref[...] = jnp.zeros_like(carry_ref)

    xt = x_ref[0]
    xb = xt.astype(BF16)
    t = s * tm + lax.broadcasted_iota(jnp.int32, (tm, 1), 0)
    y = jnp.zeros((tm, D_MODEL), F32)
    for g, w in enumerate(POOL_WINDOWS):
        lo = g * POOL_GROUP
        u = _dot(xb, w_in_ref[:, lo:lo + POOL_GROUP])
        z = _dot(xb, w_in_ref[:, D_POOL + lo:D_POOL + lo + POOL_GROUP])
        work_ref[0:POOL_HALO, :] = carry_ref[g]
        work_ref[POOL_HALO:, :] = u
        carry_ref[g] = u[tm - POOL_HALO:, :]
        tot = u
        for i in range(1, w):
            tot = tot + work_ref[POOL_HALO - i:POOL_HALO - i + tm, :]
        inv_cnt = 1.0 / jnp.minimum(t + 1, w).astype(F32)
        pooled = tot * inv_cnt - u
        m = _dot(pooled.astype(BF16), w_grp_ref[g]) * scale_ref[:, lo:lo + POOL_GROUP]
        gated = m * _silu(z)
        y = y + _dot(gated.astype(BF16), w_out_ref[lo:lo + POOL_GROUP, :])
    o_ref[0] = _layer_norm(DN_ALPHA * xt + y, g_ref[...], b_ref[...])


def _pool_layer(x, w_in, w_grp, scale, w_out, ln_g, ln_b):
    B, S, D = x.shape
    tm = POOL_TM
    return pl.pallas_call(
        _pool_layer_kernel,
        grid=(B, S // tm),
        in_specs=[
            pl.BlockSpec((1, tm, D), lambda b, s: (b, s, 0)),
            _const_spec(w_in.shape),
            _const_spec(w_grp.shape),
            _const_spec(scale.shape),
            _const_spec(w_out.shape),
            _const_spec(ln_g.shape),
            _const_spec(ln_b.shape),
        ],
        out_specs=pl.BlockSpec((1, tm, D), lambda b, s: (b, s, 0)),
        out_shape=jax.ShapeDtypeStruct((B, S, D), F32),
        scratch_shapes=[
            pltpu.VMEM((len(POOL_WINDOWS), POOL_HALO, POOL_GROUP), F32),
            pltpu.VMEM((POOL_HALO + tm, POOL_GROUP), F32),
        ],
        compiler_params=pltpu.CompilerParams(
            dimension_semantics=("parallel", "arbitrary"),
            vmem_limit_bytes=VMEM_LIMIT_BYTES),
        name="pool_layer",
    )(x, w_in, w_grp, scale, w_out, ln_g, ln_b)


def _nsa_inproj_kernel(x_ref, wq_ref, wkv_ref, wz_ref, wgl_ref,
                       q_ref, cmp_ref, kx_ref, vx_ref, z_ref, gate_ref):
    tm = x_ref.shape[1]
    xb = x_ref[0].astype(BF16)
    q_ref[0] = (_dot(xb, wq_ref[...]) * (HEAD_DIM ** -0.5 * LOG2E)).astype(BF16)
    kv = _dot(xb, wkv_ref[...])
    for a in range(2):
        for g in range(N_KV):
            lo = a * D_KV + g * HEAD_DIM
            cmp_ref[0, a, g] = kv[:, lo:lo + HEAD_DIM]

    lane = lax.broadcasted_iota(jnp.int32, (tm, LANES), 1)
    ext = lane - HEAD_DIM
    low = ext < 0
    pos = pl.program_id(1) * tm + lax.broadcasted_iota(jnp.int32, (tm, LANES), 0)
    blk = _div_pow2(pos, SEL_BLOCK)
    alibi = jnp.where(ext < N_SLOPE_PIECES, blk, jnp.where(ext < 2 * N_SLOPE_PIECES, pos - blk * SEL_BLOCK, 0))
    k_win_ext = alibi.astype(F32)
    k_sel_ext = jnp.where(ext >= EXT_ONEHOT, jnp.where(ext - EXT_ONEHOT == blk, 1.0, 0.0), k_win_ext)
    ones_a = jnp.where(low, 1.0, 0.0).astype(BF16)
    ones_b = jnp.where(low, 0.0, 1.0).astype(BF16)
    for a in range(2, 6):
        for pair in range(N_KV // 2):
            lo = a * D_KV + pair * LANES
            both = kv[:, lo:lo + LANES]
            swapped = pltpu.roll(both, HEAD_DIM, axis=1)
            for g, (at_low, at_high) in ((2 * pair, (both, swapped)), (2 * pair + 1, (swapped, both))):
                if a % 2 == 0:
                    k_ext = k_sel_ext if a == 2 else k_win_ext
                    kx_ref[0, (a - 2) // 2, g] = jnp.where(low, at_low, k_ext).astype(BF16)
                else:
                    va = jnp.where(low, at_low, 0.0).astype(BF16)
                    vb = jnp.where(low, 0.0, at_high).astype(BF16)
                    vx_ref[0, a - 3, g] = jnp.concatenate([va, ones_a], axis=1)
                    vx_ref[0, a - 2, g] = jnp.concatenate([vb, ones_b], axis=1)

    z_ref[0] = _dot(xb, wz_ref[...])
    gate_ref[0] = jax.nn.sigmoid(_dot(xb, wgl_ref[...]))


def _nsa_inproj(x, wq, wkv, wz, wgl):
    B, S, D = x.shape
    tm = PROJ_TM
    return pl.pallas_call(
        _nsa_inproj_kernel,
        grid=(B, S // tm),
        in_specs=[
            pl.BlockSpec((1, tm, D), lambda b, s: (b, s, 0)),
            _const_spec(wq.shape),
            _const_spec(wkv.shape),
            _const_spec(wz.shape),
            _const_spec(wgl.shape),
        ],
        out_specs=[
            pl.BlockSpec((1, tm, D_ATT), lambda b, s: (b, s, 0)),
            pl.BlockSpec((1, 2, N_KV, tm, HEAD_DIM), lambda b, s: (b, 0, 0, s, 0)),
            pl.BlockSpec((1, 2, N_KV, tm, LANES), lambda b, s: (b, 0, 0, s, 0)),
            pl.BlockSpec((1, 4, N_KV, tm, 2 * LANES), lambda b, s: (b, 0, 0, s, 0)),
            pl.BlockSpec((1, tm, D_ATT), lambda b, s: (b, s, 0)),
            pl.BlockSpec((1, tm, GATE_LANES), lambda b, s: (b, s, 0)),
        ],
        out_shape=[
            jax.ShapeDtypeStruct((B, S, D_ATT), BF16),
            jax.ShapeDtypeStruct((B, 2, N_KV, S, HEAD_DIM), F32),
            jax.ShapeDtypeStruct((B, 2, N_KV, S, LANES), BF16),
            jax.ShapeDtypeStruct((B, 4, N_KV, S, 2 * LANES), BF16),
            jax.ShapeDtypeStruct((B, S, D_ATT), F32),
            jax.ShapeDtypeStruct((B, S, GATE_LANES), F32),
        ],
        compiler_params=pltpu.CompilerParams(
            dimension_semantics=("parallel", "parallel"),
            vmem_limit_bytes=VMEM_LIMIT_BYTES),
        name="nsa_inproj",
    )(x, wq, wkv, wz, wgl)


def _compress_kernel(c_ref, pos_ref, w1_ref, w2_ref, o_ref):
    n_grp, n_row, width = c_ref.shape[2:]
    rows = c_ref[0, 0].reshape(n_grp * n_row, width)
    nxt = pltpu.roll(rows, shift=n_grp * n_row - 1, axis=0)
    first = (rows + pos_ref[0, 0:1, :]).astype(BF16)
    second = (nxt + pos_ref[0, 1:2, :]).astype(BF16)
    h = _dot(first, w1_ref[0, :width, :]) + _dot(second, w1_ref[0, width:, :])
    out = _dot(_silu(h).astype(BF16), w2_ref[0])
    o_ref[0, 0] = out.reshape(n_grp, n_row, HEAD_DIM).astype(BF16)


def _compress(cmp_in, pos, w1, w2):
    B = cmp_in.shape[0]
    S = cmp_in.shape[3]
    n_row = S // CMP_STRIDE
    width = CMP_STRIDE * HEAD_DIM
    c = cmp_in.reshape(B, 2, N_KV, n_row, width)
    return pl.pallas_call(
        _compress_kernel,
        grid=(2, B),
        in_specs=[
            pl.BlockSpec((1, 1, N_KV, n_row, width), lambda a, b: (b, a, 0, 0, 0)),
            pl.BlockSpec((1, 2, width), lambda a, b: (a, 0, 0)),
            pl.BlockSpec((1,) + w1.shape[1:], lambda a, b: (a, 0, 0)),
            pl.BlockSpec((1,) + w2.shape[1:], lambda a, b: (a, 0, 0)),
        ],
        out_specs=pl.BlockSpec((1, 1, N_KV, n_row, HEAD_DIM), lambda a, b: (b, a, 0, 0, 0)),
        out_shape=jax.ShapeDtypeStruct((B, 2, N_KV, n_row, HEAD_DIM), BF16),
        compiler_params=pltpu.CompilerParams(
            dimension_semantics=("parallel", "parallel"),
            vmem_limit_bytes=VMEM_LIMIT_BYTES),
        name="nsa_compress",
    )(c, pos, w1, w2)


def _attn_kernel(slope_ref, q_ref, cmp_ref, kx_ref, vx_ref, gate_ref, expand_ref, o_ref, m_ref, acc_ref):
    tq = q_ref.shape[1]
    n_cmp_pad = cmp_ref.shape[3]
    seq = kx_ref.shape[3]
    n_sel = seq // SEL_BLOCK
    kb = ATT_KB
    n_pair = HPG // 2
    grp = pl.program_id(1)
    step = pl.program_id(2)
    q0 = step * tq

    lane = lax.broadcasted_iota(jnp.int32, (1, LANES), 1)
    low = lane < HEAD_DIM
    first_head = jnp.bitwise_and(lax.broadcasted_iota(jnp.int32, (1, 2 * LANES), 1), HEAD_DIM) == 0
    t_col = q0 + lax.broadcasted_iota(jnp.int32, (tq, 1), 0)

    qf = q_ref[0].astype(F32)
    q_src = []
    for pair in range(n_pair):
        both = qf[:, pair * LANES:(pair + 1) * LANES]
        q_src += [both, pltpu.roll(both, HEAD_DIM, axis=1)]

    k_cmp = cmp_ref[0, 0, 0].astype(F32)
    v_cmp = cmp_ref[0, 1, 0].astype(F32)
    zero_half = jnp.zeros_like(v_cmp)
    k_cmp_x = jnp.concatenate([k_cmp, zero_half], axis=1).astype(BF16)
    v_cmp_x = (jnp.concatenate([v_cmp, zero_half], axis=1).astype(BF16),
               jnp.concatenate([zero_half, v_cmp], axis=1).astype(BF16))
    c_row = lax.broadcasted_iota(jnp.int32, (1, n_cmp_pad), 1)
    dist_c = t_col - (c_row * CMP_STRIDE + (CMP_BLOCK - 1))
    valid_c = dist_c >= 0
    dist_cf = dist_c.astype(F32)
    p_sum = jnp.zeros((tq, n_cmp_pad), F32)
    o_cmp = []
    for pair in range(n_pair):
        o_pair = None
        for hh in range(2):
            h = 2 * pair + hh
            slope = slope_ref[(grp * HPG + h) * SLOPE_ROW + 2 * N_SLOPE_PIECES]
            s = _dot_nt(q_src[h].astype(BF16), k_cmp_x) - slope * dist_cf
            s = jnp.where(valid_c, s, NEG_INF)
            m = jnp.max(s, axis=-1, keepdims=True)
            p = jnp.where(valid_c, jnp.exp2(s - m), 0.0)
            l = jnp.sum(p, axis=-1, keepdims=True)
            p = p * jnp.where(l > 0.0, 1.0 / l, 0.0)
            p_sum = p_sum + p
            d = _dot(p.astype(BF16), v_cmp_x[hh])
            o_pair = d if o_pair is None else o_pair + d
        o_cmp.append(o_pair)

    j_ov = lax.broadcasted_iota(jnp.int32, (n_sel, n_cmp_pad), 0) * SEL_BLOCK
    c_ov = lax.broadcasted_iota(jnp.int32, (n_sel, n_cmp_pad), 1) * CMP_STRIDE
    overlap_t = jnp.where(c_ov < j_ov + SEL_BLOCK, jnp.where(c_ov + CMP_BLOCK > j_ov, 1.0, 0.0), 0.0).astype(BF16)
    p_hi = p_sum.astype(BF16)
    p_lo = (p_sum - p_hi.astype(F32)).astype(BF16)
    imp = _dot_nt(overlap_t, p_hi) + _dot_nt(overlap_t, p_lo)
    j_blk = lax.broadcasted_iota(jnp.int32, (n_sel, tq), 0)
    cur = _div_pow2(q0 + lax.broadcasted_iota(jnp.int32, (n_sel, tq), 1), SEL_BLOCK)
    forced = (j_blk == 0) | (j_blk == cur) | (j_blk == cur - 1)
    imp = jnp.where(forced, FORCE_SCORE, jnp.where(j_blk > cur, NEG_INF, imp))
    n_tile = n_sel // SUBLANES
    tiles = [imp[v * SUBLANES:(v + 1) * SUBLANES, :] for v in range(n_tile)]
    j_loc = lax.broadcasted_iota(jnp.int32, (SUBLANES, tq), 0)
    rank = [jnp.zeros((SUBLANES, tq), F32) for _ in range(n_tile)]
    for i in range(n_sel):
        v, i_loc = divmod(i, SUBLANES)
        row = tiles[v][i_loc:i_loc + 1, :]
        for w in range(n_tile):
            if w > v:
                ahead = jnp.where(row >= tiles[w], 1.0, 0.0)
            elif w < v:
                ahead = jnp.where(row > tiles[w], 1.0, 0.0)
            else:
                ahead = jnp.where(j_loc > i_loc, jnp.where(row >= tiles[w], 1.0, 0.0),
                                  jnp.where(row > tiles[w], 1.0, 0.0))
            rank[w] = rank[w] + ahead
    top_n = float(min(SEL_TOPN, n_sel))
    assert EXT_LANES - EXT_ONEHOT == n_sel
    mask_rows = [jnp.zeros((HEAD_DIM + EXT_ONEHOT, tq), F32)] + [jnp.where(r < top_n, 0.0, NEG_INF) for r in rank]
    mask_ext = jnp.transpose(jnp.concatenate(mask_rows, axis=0))

    q_aug, q_win = [], []
    for h in range(HPG):
        base = (grp * HPG + h) * SLOPE_ROW
        slope_row = jnp.zeros((1, LANES), F32)
        for j in range(2 * N_SLOPE_PIECES):
            slope_row = jnp.where(lane == HEAD_DIM + j, slope_ref[base + j], slope_row)
        q_aug.append(jnp.where(low, q_src[h], mask_ext + slope_row).astype(BF16))
        q_win.append(jnp.where(low, q_src[h], slope_row).astype(BF16))

    def tile_lanes(x, n):
        return jnp.concatenate([x] * n, axis=1)

    def chunk_init(queries, which_k, v_base, k0, bias):
        k_c = kx_ref[0, which_k, 0, pl.ds(k0, kb), :]
        ms, accs = [], []
        for pair in range(n_pair):
            pv = None
            for hh in range(2):
                s = _dot_nt(queries[2 * pair + hh], k_c) + bias
                m = jnp.max(s, axis=-1, keepdims=True)
                p = jnp.exp2(s - m).astype(BF16)
                d = _dot(p, vx_ref[0, v_base + hh, 0, pl.ds(k0, kb), :])
                pv = d if pv is None else pv + d
                ms.append(m)
            accs.append(pv)
        return ms, accs

    assert kb - tq == WINDOW
    k0 = pl.multiple_of(jnp.maximum(q0 - WINDOW, 0), LANES)
    pos = k0 + lax.broadcasted_iota(jnp.int32, (1, kb), 1)
    n_loop = _div_pow2(step * tq, kb)
    causal = pos <= t_col
    sel_bias = jnp.where(causal, jnp.where(pos >= n_loop * kb, 0.0, NEG_INF), NEG_INF)
    win_bias = jnp.where(causal, jnp.where(pos > t_col - WINDOW, 0.0, NEG_INF), NEG_INF)

    _, acc_win = chunk_init(q_win, 1, 2, k0, win_bias)

    m_sel, acc_sel = chunk_init(q_aug, 0, 0, k0, sel_bias)
    for h in range(HPG):
        m_ref[h] = jnp.broadcast_to(m_sel[h], (tq, LANES))
    for pair in range(n_pair):
        acc_ref[pair] = acc_sel[pair]

    def sel_step(c, carry):
        kc0 = pl.multiple_of(c * kb, kb)
        k_c = kx_ref[0, 0, 0, pl.ds(kc0, kb), :]
        for pair in range(n_pair):
            pv, scale = None, []
            for hh in range(2):
                h = 2 * pair + hh
                s = _dot_nt(q_aug[h], k_c)
                m_old = m_ref[h]
                m_new = jnp.maximum(m_old, jnp.max(s, axis=-1, keepdims=True))
                m_ref[h] = m_new
                scale.append(tile_lanes(jnp.exp2(m_old - m_new), 2))
                p = jnp.exp2(s - tile_lanes(m_new, kb // LANES)).astype(BF16)
                d = _dot(p, vx_ref[0, hh, 0, pl.ds(kc0, kb), :])
                pv = d if pv is None else pv + d
            acc_ref[pair] = jnp.where(first_head, scale[0], scale[1]) * acc_ref[pair] + pv
        return carry

    lax.fori_loop(0, n_loop, sel_step, 0)

    gates = gate_ref[0]
    g_hi = gates.astype(BF16)
    g_lo = (gates - g_hi.astype(F32)).astype(BF16)
    g_exp = _dot(jnp.concatenate([g_hi, g_lo], axis=1), expand_ref[0])
    outs = []
    for pair in range(n_pair):
        a_sel = acc_ref[pair]
        a_win = acc_win[pair]
        lo = pair * LANES
        width = HPG * HEAD_DIM
        outs.append(g_exp[:, lo:lo + LANES] * o_cmp[pair]
                    + g_exp[:, width + lo:width + lo + LANES] * (a_sel[:, :LANES] / a_sel[:, LANES:])
                    + g_exp[:, 2 * width + lo:2 * width + lo + LANES] * (a_win[:, :LANES] / a_win[:, LANES:]))
    o_ref[0] = jnp.concatenate(outs, axis=1)


def _gate_expansion():
    r = jnp.arange(2 * GATE_LANES)[None, :, None] % GATE_LANES
    c = jnp.arange(N_BRANCH * HPG * HEAD_DIM)[None, None, :]
    g = jnp.arange(N_KV)[:, None, None]
    br, h = c // (HPG * HEAD_DIM), (c // HEAD_DIM) % HPG
    return (r == g * HPG * N_BRANCH + h * N_BRANCH + br).astype(BF16)


def _attention(slope_table, q, cmp_kv, kx, vx, gates):
    B, S, _ = q.shape
    tq = ATT_TQ
    n_cmp_pad = cmp_kv.shape[3]
    expand = _gate_expansion()
    grid_spec = pltpu.PrefetchScalarGridSpec(
        num_scalar_prefetch=1,
        grid=(B, N_KV, S // tq),
        in_specs=[
            pl.BlockSpec((1, tq, HPG * HEAD_DIM), lambda b, g, i, sl: (b, i, g)),
            pl.BlockSpec((1, 2, 1, n_cmp_pad, HEAD_DIM), lambda b, g, i, sl: (b, 0, g, 0, 0)),
            pl.BlockSpec((1, 2, 1, S, LANES), lambda b, g, i, sl: (b, 0, g, 0, 0)),
            pl.BlockSpec((1, 4, 1, S, 2 * LANES), lambda b, g, i, sl: (b, 0, g, 0, 0)),
            pl.BlockSpec((1, tq, GATE_LANES), lambda b, g, i, sl: (b, i, 0)),
            pl.BlockSpec((1,) + expand.shape[1:], lambda b, g, i, sl: (g, 0, 0)),
        ],
        out_specs=pl.BlockSpec((1, tq, HPG * HEAD_DIM), lambda b, g, i, sl: (b, i, g)),
        scratch_shapes=[
            pltpu.VMEM((HPG, tq, LANES), F32),
            pltpu.VMEM((HPG // 2, tq, 2 * LANES), F32),
        ],
    )
    return pl.pallas_call(
        _attn_kernel,
        grid_spec=grid_spec,
        out_shape=jax.ShapeDtypeStruct((B, S, D_ATT), F32),
        compiler_params=pltpu.CompilerParams(
            dimension_semantics=("parallel", "parallel", "arbitrary"),
            vmem_limit_bytes=VMEM_LIMIT_BYTES),
        name="nsa_attention",
    )(slope_table, q, cmp_kv, kx, vx, gates, expand)


def _attn_kernel_old(slope_ref, q_ref, cmp_ref, kvs_ref, gate_ref, o_ref):
    tq = q_ref.shape[1]
    n_cmp_pad = cmp_ref.shape[3]
    seq = kvs_ref.shape[3]
    n_sel = seq // SEL_BLOCK
    kb = tq
    grp = pl.program_id(1)
    step = pl.program_id(2)
    q0 = step * tq

    qf = q_ref[0].astype(F32)
    qh = [qf[:, h * HEAD_DIM:(h + 1) * HEAD_DIM] for h in range(HPG)]
    t_col = q0 + lax.broadcasted_iota(jnp.int32, (tq, 1), 0)

    k_cmp = cmp_ref[0, 0, 0]
    v_cmp = cmp_ref[0, 1, 0]
    c_row = lax.broadcasted_iota(jnp.int32, (1, n_cmp_pad), 1)
    dist_c = t_col - (c_row * CMP_STRIDE + (CMP_BLOCK - 1))
    valid_c = dist_c >= 0
    dist_cf = dist_c.astype(F32)
    p_sum = jnp.zeros((tq, n_cmp_pad), F32)
    o_cmp = []
    for h in range(HPG):
        slope = slope_ref[(grp * HPG + h) * SLOPE_ROW + 2 * N_SLOPE_PIECES]
        s = _dot_nt(qh[h].astype(BF16), k_cmp) - slope * dist_cf
        s = jnp.where(valid_c, s, NEG_INF)
        m = jnp.max(s, axis=-1, keepdims=True)
        p = jnp.where(valid_c, jnp.exp2(s - m), 0.0)
        l = jnp.sum(p, axis=-1, keepdims=True)
        p = p * jnp.where(l > 0.0, 1.0 / l, 0.0)
        p_sum = p_sum + p
        o_cmp.append(_dot(p.astype(BF16), v_cmp))

    j_ov = lax.broadcasted_iota(jnp.int32, (n_sel, n_cmp_pad), 0) * SEL_BLOCK
    c_ov = lax.broadcasted_iota(jnp.int32, (n_sel, n_cmp_pad), 1) * CMP_STRIDE
    overlap_t = jnp.where((c_ov < j_ov + SEL_BLOCK) & (c_ov + CMP_BLOCK > j_ov), 1.0, 0.0).astype(BF16)
    p_hi = p_sum.astype(BF16)
    p_lo = (p_sum - p_hi.astype(F32)).astype(BF16)
    imp = _dot_nt(overlap_t, p_hi) + _dot_nt(overlap_t, p_lo)
    j_blk = lax.broadcasted_iota(jnp.int32, (n_sel, tq), 0)
    cur = _div_pow2(q0 + lax.broadcasted_iota(jnp.int32, (n_sel, tq), 1), SEL_BLOCK)
    forced = (j_blk == 0) | (j_blk == cur) | (j_blk == cur - 1)
    imp = jnp.where(forced, FORCE_SCORE, jnp.where(j_blk > cur, NEG_INF, imp))
    rank = jnp.zeros((n_sel, tq), jnp.int32)
    for i in range(n_sel):
        row = imp[i:i + 1, :]
        rank = rank + jnp.where(j_blk > i, jnp.where(row >= imp, 1, 0), jnp.where(row > imp, 1, 0))
    mask_t = jnp.where(rank < min(SEL_TOPN, n_sel), 0.0, NEG_INF)
    assert EXT_LANES - EXT_ONEHOT == n_sel
    mask_ext = jnp.transpose(jnp.concatenate([jnp.zeros((EXT_ONEHOT, tq), F32), mask_t], axis=0))

    lane_ext = lax.broadcasted_iota(jnp.int32, (1, EXT_LANES), 1)
    q_aug = []
    for h in range(HPG):
        base = (grp * HPG + h) * SLOPE_ROW
        slope_ext = jnp.zeros((1, EXT_LANES), F32)
        for j in range(2 * N_SLOPE_PIECES):
            slope_ext = jnp.where(lane_ext == j, slope_ref[base + j], slope_ext)
        q_aug.append(jnp.concatenate([qh[h], mask_ext + slope_ext], axis=-1).astype(BF16))
    q_all = jnp.concatenate(q_aug, axis=0)

    def add_bias(s, bias):
        return (s.reshape(HPG, tq, -1) + bias[None]).reshape(HPG * tq, -1)

    def sel_update(carry, k0, bias):
        m_old, acc_old = carry
        k_c = kvs_ref[0, 0, 0, pl.ds(k0, kb), :]
        v_c = kvs_ref[0, 1, 0, pl.ds(k0, kb), :]
        s = _dot_nt(q_all, k_c)
        if bias is not None:
            s = add_bias(s, bias)
        m_new = jnp.maximum(m_old, jnp.max(s, axis=-1, keepdims=True))
        p = jnp.exp2(s - m_new)
        acc_new = jnp.exp2(m_old - m_new) * acc_old + _dot(p.astype(BF16), v_c)
        return m_new, acc_new

    init = (jnp.full((HPG * tq, 1), NEG_INF, F32), jnp.zeros((HPG * tq, LANES), F32))
    past = lax.fori_loop(0, step, lambda c, carry: sel_update(carry, pl.multiple_of(c * kb, kb), None), init)
    r_col = lax.broadcasted_iota(jnp.int32, (tq, kb), 0)
    r_row = lax.broadcasted_iota(jnp.int32, (tq, kb), 1)
    causal_bias = jnp.where(r_row <= r_col, 0.0, NEG_INF)
    _, acc_sel = sel_update(past, pl.multiple_of(q0, kb), causal_bias)

    n_win = WINDOW + tq
    w0 = pl.multiple_of(jnp.maximum(q0 - WINDOW, 0), LANES)
    k_w = kvs_ref[0, 2, 0, pl.ds(w0, n_win), :]
    v_w = kvs_ref[0, 3, 0, pl.ds(w0, n_win), :]
    dist_w = t_col - (w0 + lax.broadcasted_iota(jnp.int32, (1, n_win), 1))
    win_bias = jnp.where((dist_w >= 0) & (dist_w < WINDOW), 0.0, NEG_INF)
    s = add_bias(_dot_nt(q_all, k_w), win_bias)
    p = jnp.exp2(s - jnp.max(s, axis=-1, keepdims=True))
    acc_win = _dot(p.astype(BF16), v_w)

    gates = gate_ref[0, 0]
    outs = []
    for h in range(HPG):
        rows = slice(h * tq, (h + 1) * tq)
        o_sel = acc_sel[rows, :HEAD_DIM] / acc_sel[rows, HEAD_DIM:HEAD_DIM + 1]
        o_win = acc_win[rows, :HEAD_DIM] / acc_win[rows, HEAD_DIM:HEAD_DIM + 1]
        gb = gates[:, h * N_BRANCH:(h + 1) * N_BRANCH]
        outs.append(gb[:, 0:1] * o_cmp[h] + gb[:, 1:2] * o_sel + gb[:, 2:3] * o_win)
    o_ref[0] = jnp.concatenate(outs, axis=-1)


def _attention_old(slope_table, q, cmp_kv, kvs, gates):
    B, S, _ = q.shape
    tq = ATT_TQ
    n_cmp_pad = cmp_kv.shape[3]
    per_group = HPG * N_BRANCH
    grid_spec = pltpu.PrefetchScalarGridSpec(
        num_scalar_prefetch=1,
        grid=(B, N_KV, S // tq),
        in_specs=[
            pl.BlockSpec((1, tq, HPG * HEAD_DIM), lambda b, g, i, sl: (b, i, g)),
            pl.BlockSpec((1, 2, 1, n_cmp_pad, HEAD_DIM), lambda b, g, i, sl: (b, 0, g, 0, 0)),
            pl.BlockSpec((1, 4, 1, S, LANES), lambda b, g, i, sl: (b, 0, g, 0, 0)),
            pl.BlockSpec((1, 1, tq, per_group), lambda b, g, i, sl: (b, g, i, 0)),
        ],
        out_specs=pl.BlockSpec((1, tq, HPG * HEAD_DIM), lambda b, g, i, sl: (b, i, g)),
    )
    return pl.pallas_call(
        _attn_kernel,
        grid_spec=grid_spec,
        out_shape=jax.ShapeDtypeStruct((B, S, D_ATT), F32),
        compiler_params=pltpu.CompilerParams(
            dimension_semantics=("parallel", "parallel", "arbitrary"),
            vmem_limit_bytes=VMEM_LIMIT_BYTES),
        name="nsa_attention",
    )(slope_table, q, cmp_kv, kvs, gates)


def _nsa_out_kernel(o_ref, z_ref, x_ref, w_ref, g_ref, b_ref, out_ref):
    gated = o_ref[0] * _silu(z_ref[0])
    y = _dot(gated.astype(BF16), w_ref[...])
    out_ref[0] = _layer_norm(DN_ALPHA * x_ref[0] + y, g_ref[...], b_ref[...])


def _nsa_out(o, z, x, w_out, ln_g, ln_b):
    B, S, D = x.shape
    tm = PROJ_TM
    tile = pl.BlockSpec((1, tm, D), lambda b, s: (b, s, 0))
    return pl.pallas_call(
        _nsa_out_kernel,
        grid=(B, S // tm),
        in_specs=[tile, tile, tile, _const_spec(w_out.shape), _const_spec(ln_g.shape), _const_spec(ln_b.shape)],
        out_specs=tile,
        out_shape=jax.ShapeDtypeStruct((B, S, D), F32),
        compiler_params=pltpu.CompilerParams(
            dimension_semantics=("parallel", "parallel"),
            vmem_limit_bytes=VMEM_LIMIT_BYTES),
        name="nsa_out",
    )(o, z, x, w_out, ln_g, ln_b)


def _slope_table():
    h = jnp.arange(1, N_HEADS + 1, dtype=F32)
    slope = (2.0 ** (-8.0 * h / N_HEADS)) * LOG2E
    pieces, rest = [], slope
    for _ in range(N_SLOPE_PIECES):
        piece = rest.astype(BF16).astype(F32)
        pieces.append(piece)
        rest = rest - piece
    cols = [SEL_BLOCK * p for p in pieces] + pieces + [slope]
    cols += [jnp.zeros_like(slope)] * (SLOPE_ROW - len(cols))
    return jnp.stack(cols, axis=1).reshape(-1)


def kernel(x, ln_g, ln_b, pool_w_in, pool_w_grp, pool_scale, pool_w_out, nsa_w_in, nsa_cmp_pos_k, nsa_cmp_w1_k,
           nsa_cmp_w2_k, nsa_cmp_pos_v, nsa_cmp_w1_v, nsa_cmp_w2_v, nsa_w_out):
    seq = x.shape[1]
    assert seq % max(POOL_TM, PROJ_TM, ATT_TQ) == 0 and x.shape[2] == D_MODEL
    assert seq >= WINDOW + ATT_TQ and WINDOW % ATT_TQ == 0 and ATT_TQ % SEL_BLOCK == 0
    assert seq // SEL_BLOCK == EXT_LANES - EXT_ONEHOT

    x = _pool_layer(x, pool_w_in[0].astype(BF16), pool_w_grp[0].astype(BF16), pool_scale[0][None, :],
                    pool_w_out[0].astype(BF16), ln_g[0][None, :], ln_b[0][None, :])

    w = nsa_w_in[0]
    kv_lo, z_lo, gl_lo = D_ATT, D_ATT + 6 * D_KV, 2 * D_ATT + 6 * D_KV
    wq = w[:, :kv_lo].astype(BF16)
    wkv = w[:, kv_lo:z_lo].astype(BF16)
    wz = w[:, z_lo:gl_lo].astype(BF16)
    wgl = jnp.pad(w[:, gl_lo:], ((0, 0), (0, GATE_LANES - N_BRANCH * N_HEADS))).astype(BF16)
    q, cmp_in, kx, vx, z, gates = _nsa_inproj(x, wq, wkv, wz, wgl)

    half = CMP_BLOCK // 2
    pos = jnp.stack([nsa_cmp_pos_k[0], nsa_cmp_pos_v[0]]).reshape(2, 2, half * HEAD_DIM)
    w1 = jnp.stack([nsa_cmp_w1_k[0], nsa_cmp_w1_v[0]]).astype(BF16)
    w2 = jnp.stack([nsa_cmp_w2_k[0], nsa_cmp_w2_v[0]]).astype(BF16)
    cmp_kv = _compress(cmp_in, pos, w1, w2)

    o = _attention(_slope_table(), q, cmp_kv, kx, vx, gates)
    return _nsa_out(o, z, x, nsa_w_out[0].astype(BF16), ln_g[1][None, :], ln_b[1][None, :])
```

```python
import math

import jax
import jax.numpy as jnp
from jax import lax
from jax.experimental import pallas as pl
from jax.experimental.pallas import tpu as pltpu

F32 = jnp.float32
BF16 = jnp.bfloat16

D_MODEL = 1024
DEPTH = 2
D_POOL = 2 * D_MODEL
POOL_WINDOWS = (2, 4, 8, 16)
POOL_GROUP = D_POOL // len(POOL_WINDOWS)
HEAD_DIM = 64
N_HEADS = D_MODEL // HEAD_DIM
N_KV = 4
HPG = N_HEADS // N_KV
D_ATT = N_HEADS * HEAD_DIM
D_KV = N_KV * HEAD_DIM
N_BRANCH = 3
CMP_BLOCK = 32
CMP_STRIDE = 16
SEL_BLOCK = 64
SEL_TOPN = 8
WINDOW = 256
DN_ALPHA = (2.0 * DEPTH) ** 0.25
LN_EPS = 1e-5
NEG_INF = -1e30
FORCE_SCORE = 1e9
LOG2E = math.log2(math.e)

LANES = 128
SUBLANES = 8
VMEM_LIMIT_BYTES = 56 * 1024 * 1024

POOL_TM = 512
POOL_HALO = 16
PROJ_TM = 512
ATT_TQ = 256
ATT_KB = 512
GATE_LANES = LANES

N_SLOPE_PIECES = 3
EXT_ONEHOT = 32
SLOPE_ROW = 8
EXT_LANES = LANES - HEAD_DIM


def _dot(a, b):
    return jnp.dot(a, b, preferred_element_type=F32)


def _dot_nt(a, b):
    return lax.dot_general(a, b, (((1,), (1,)), ((), ())), preferred_element_type=F32)


def _div_pow2(x, n):
    assert n & (n - 1) == 0
    return lax.shift_right_logical(x, n.bit_length() - 1)


def _silu(x):
    return x * jax.nn.sigmoid(x)


def _layer_norm(r, g, b):
    mu = jnp.mean(r, axis=-1, keepdims=True)
    d = r - mu
    var = jnp.mean(d * d, axis=-1, keepdims=True)
    return d * lax.rsqrt(var + LN_EPS) * g + b


def _const_spec(shape):
    n = len(shape)
    return pl.BlockSpec(shape, lambda *_: (0,) * n, pipeline_mode=pl.Buffered(1))


def _pool_layer_kernel(x_ref, w_in_ref, w_grp_ref, scale_ref, w_out_ref, g_ref, b_ref,
                       o_ref, carry_ref, work_ref):
    tm = x_ref.shape[1]
    s = pl.program_id(1)

    @pl.when(s == 0)
    def _():
        carry_ref[...] = jnp.zeros_like(carry_ref)

    xt = x_ref[0]
    xb = xt.astype(BF16)
    t = s * tm + lax.broadcasted_iota(jnp.int32, (tm, 1), 0)
    y = jnp.zeros((tm, D_MODEL), F32)
    for g, w in enumerate(POOL_WINDOWS):
        lo = g * POOL_GROUP
        u = _dot(xb, w_in_ref[:, lo:lo + POOL_GROUP])
        z = _dot(xb, w_in_ref[:, D_POOL + lo:D_POOL + lo + POOL_GROUP])
        work_ref[0:POOL_HALO, :] = carry_ref[g]
        work_ref[POOL_HALO:, :] = u
        carry_ref[g] = u[tm - POOL_HALO:, :]
        tot = u
        for i in range(1, w):
            tot = tot + work_ref[POOL_HALO - i:POOL_HALO - i + tm, :]
        inv_cnt = 1.0 / jnp.minimum(t + 1, w).astype(F32)
        pooled = tot * inv_cnt - u
        m = _dot(pooled.astype(BF16), w_grp_ref[g]) * scale_ref[:, lo:lo + POOL_GROUP]
        gated = m * _silu(z)
        y = y + _dot(gated.astype(BF16), w_out_ref[lo:lo + POOL_GROUP, :])
    o_ref[0] = _layer_norm(DN_ALPHA * xt + y, g_ref[...], b_ref[...])


def _pool_layer(x, w_in, w_grp, scale, w_out, ln_g, ln_b):
    B, S, D = x.shape
    tm = POOL_TM
    return pl.pallas_call(
        _pool_layer_kernel,
        grid=(B, S // tm),
        in_specs=[
            pl.BlockSpec((1, tm, D), lambda b, s: (b, s, 0)),
            _const_spec(w_in.shape),
            _const_spec(w_grp.shape),
            _const_spec(scale.shape),
            _const_spec(w_out.shape),
            _const_spec(ln_g.shape),
            _const_spec(ln_b.shape),
        ],
        out_specs=pl.BlockSpec((1, tm, D), lambda b, s: (b, s, 0)),
        out_shape=jax.ShapeDtypeStruct((B, S, D), F32),
        scratch_shapes=[
            pltpu.VMEM((len(POOL_WINDOWS), POOL_HALO, POOL_GROUP), F32),
            pltpu.VMEM((POOL_HALO + tm, POOL_GROUP), F32),
        ],
        compiler_params=pltpu.CompilerParams(
            dimension_semantics=("parallel", "arbitrary"),
            vmem_limit_bytes=VMEM_LIMIT_BYTES),
        name="pool_layer",
    )(x, w_in, w_grp, scale, w_out, ln_g, ln_b)


def _nsa_inproj_kernel(x_ref, wq_ref, wkv_ref, wz_ref, wgl_ref,
                       q_ref, cmp_ref, kx_ref, vx_ref, z_ref, gate_ref):
    tm = x_ref.shape[1]
    xb = x_ref[0].astype(BF16)
    q_ref[0] = (_dot(xb, wq_ref[...]) * (HEAD_DIM ** -0.5 * LOG2E)).astype(BF16)
    kv = _dot(xb, wkv_ref[...])
    for a in range(2):
        for g in range(N_KV):
            lo = a * D_KV + g * HEAD_DIM
            cmp_ref[0, a, g] = kv[:, lo:lo + HEAD_DIM]

    lane = lax.broadcasted_iota(jnp.int32, (tm, LANES), 1)
    ext = lane - HEAD_DIM
    low = ext < 0
    pos = pl.program_id(1) * tm + lax.broadcasted_iota(jnp.int32, (tm, LANES), 0)
    blk = _div_pow2(pos, SEL_BLOCK)
    alibi = jnp.where(ext < N_SLOPE_PIECES, blk, jnp.where(ext < 2 * N_SLOPE_PIECES, pos - blk * SEL_BLOCK, 0))
    k_win_ext = alibi.astype(F32)
    k_sel_ext = jnp.where(ext >= EXT_ONEHOT, jnp.where(ext - EXT_ONEHOT == blk, 1.0, 0.0), k_win_ext)
    ones_a = jnp.where(low, 1.0, 0.0).astype(BF16)
    ones_b = jnp.where(low, 0.0, 1.0).astype(BF16)
    for a in range(2, 6):
        for pair in range(N_KV // 2):
            lo = a * D_KV + pair * LANES
            both = kv[:, lo:lo + LANES]
            swapped = pltpu.roll(both, HEAD_DIM, axis=1)
            for g, (at_low, at_high) in ((2 * pair, (both, swapped)), (2 * pair + 1, (swapped, both))):
                if a % 2 == 0:
                    k_ext = k_sel_ext if a == 2 else k_win_ext
                    kx_ref[0, (a - 2) // 2, g] = jnp.where(low, at_low, k_ext).astype(BF16)
                else:
                    va = jnp.where(low, at_low, 0.0).astype(BF16)
                    vb = jnp.where(low, 0.0, at_high).astype(BF16)
                    vx_ref[0, a - 3, g] = jnp.concatenate([va, ones_a], axis=1)
                    vx_ref[0, a - 2, g] = jnp.concatenate([vb, ones_b], axis=1)

    z_ref[0] = _dot(xb, wz_ref[...])
    gate_ref[0] = jax.nn.sigmoid(_dot(xb, wgl_ref[...]))


def _nsa_inproj(x, wq, wkv, wz, wgl):
    B, S, D = x.shape
    tm = PROJ_TM
    return pl.pallas_call(
        _nsa_inproj_kernel,
        grid=(B, S // tm),
        in_specs=[
            pl.BlockSpec((1, tm, D), lambda b, s: (b, s, 0)),
            _const_spec(wq.shape),
            _const_spec(wkv.shape),
            _const_spec(wz.shape),
            _const_spec(wgl.shape),
        ],
        out_specs=[
            pl.BlockSpec((1, tm, D_ATT), lambda b, s: (b, s, 0)),
            pl.BlockSpec((1, 2, N_KV, tm, HEAD_DIM), lambda b, s: (b, 0, 0, s, 0)),
            pl.BlockSpec((1, 2, N_KV, tm, LANES), lambda b, s: (b, 0, 0, s, 0)),
            pl.BlockSpec((1, 4, N_KV, tm, 2 * LANES), lambda b, s: (b, 0, 0, s, 0)),
            pl.BlockSpec((1, tm, D_ATT), lambda b, s: (b, s, 0)),
            pl.BlockSpec((1, tm, GATE_LANES), lambda b, s: (b, s, 0)),
        ],
        out_shape=[
            jax.ShapeDtypeStruct((B, S, D_ATT), BF16),
            jax.ShapeDtypeStruct((B, 2, N_KV, S, HEAD_DIM), F32),
            jax.ShapeDtypeStruct((B, 2, N_KV, S, LANES), BF16),
            jax.ShapeDtypeStruct((B, 4, N_KV, S, 2 * LANES), BF16),
            jax.ShapeDtypeStruct((B, S, D_ATT), F32),
            jax.ShapeDtypeStruct((B, S, GATE_LANES), F32),
        ],
        compiler_params=pltpu.CompilerParams(
            dimension_semantics=("parallel", "parallel"),
            vmem_limit_bytes=VMEM_LIMIT_BYTES),
        name="nsa_inproj",
    )(x, wq, wkv, wz, wgl)


def _compress_kernel(c_ref, pos_ref, w1_ref, w2_ref, o_ref):
    n_grp, n_row, width = c_ref.shape[2:]
    rows = c_ref[0, 0].reshape(n_grp * n_row, width)
    nxt = pltpu.roll(rows, shift=n_grp * n_row - 1, axis=0)
    first = (rows + pos_ref[0, 0:1, :]).astype(BF16)
    second = (nxt + pos_ref[0, 1:2, :]).astype(BF16)
    h = _dot(first, w1_ref[0, :width, :]) + _dot(second, w1_ref[0, width:, :])
    out = _dot(_silu(h).astype(BF16), w2_ref[0])
    o_ref[0, 0] = out.reshape(n_grp, n_row, HEAD_DIM).astype(BF16)


def _compress(cmp_in, pos, w1, w2):
    B = cmp_in.shape[0]
    S = cmp_in.shape[3]
    n_row = S // CMP_STRIDE
    width = CMP_STRIDE * HEAD_DIM
    c = cmp_in.reshape(B, 2, N_KV, n_row, width)
    return pl.pallas_call(
        _compress_kernel,
        grid=(2, B),
        in_specs=[
            pl.BlockSpec((1, 1, N_KV, n_row, width), lambda a, b: (b, a, 0, 0, 0)),
            pl.BlockSpec((1, 2, width), lambda a, b: (a, 0, 0)),
            pl.BlockSpec((1,) + w1.shape[1:], lambda a, b: (a, 0, 0)),
            pl.BlockSpec((1,) + w2.shape[1:], lambda a, b: (a, 0, 0)),
        ],
        out_specs=pl.BlockSpec((1, 1, N_KV, n_row, HEAD_DIM), lambda a, b: (b, a, 0, 0, 0)),
        out_shape=jax.ShapeDtypeStruct((B, 2, N_KV, n_row, HEAD_DIM), BF16),
        compiler_params=pltpu.CompilerParams(
            dimension_semantics=("parallel", "parallel"),
            vmem_limit_bytes=VMEM_LIMIT_BYTES),
        name="nsa_compress",
    )(c, pos, w1, w2)


def _attn_kernel(slope_ref, q_ref, cmp_ref, kx_ref, vx_ref, gate_ref, expand_ref, o_ref, m_ref, acc_ref):
    tq = q_ref.shape[1]
    n_cmp_pad = cmp_ref.shape[3]
    seq = kx_ref.shape[3]
    n_sel = seq // SEL_BLOCK
    kb = ATT_KB
    n_pair = HPG // 2
    grp = pl.program_id(1)
    step = pl.program_id(2)
    q0 = step * tq

    lane = lax.broadcasted_iota(jnp.int32, (1, LANES), 1)
    low = lane < HEAD_DIM
    first_head = jnp.bitwise_and(lax.broadcasted_iota(jnp.int32, (1, 2 * LANES), 1), HEAD_DIM) == 0
    t_col = q0 + lax.broadcasted_iota(jnp.int32, (tq, 1), 0)

    qf = q_ref[0].astype(F32)
    q_src = []
    for pair in range(n_pair):
        both = qf[:, pair * LANES:(pair + 1) * LANES]
        q_src += [both, pltpu.roll(both, HEAD_DIM, axis=1)]

    k_cmp = cmp_ref[0, 0, 0].astype(F32)
    v_cmp = cmp_ref[0, 1, 0].astype(F32)
    zero_half = jnp.zeros_like(v_cmp)
    k_cmp_x = jnp.concatenate([k_cmp, zero_half], axis=1).astype(BF16)
    v_cmp_x = (jnp.concatenate([v_cmp, zero_half], axis=1).astype(BF16),
               jnp.concatenate([zero_half, v_cmp], axis=1).astype(BF16))
    c_row = lax.broadcasted_iota(jnp.int32, (1, n_cmp_pad), 1)
    dist_c = t_col - (c_row * CMP_STRIDE + (CMP_BLOCK - 1))
    valid_c = dist_c >= 0
    dist_cf = dist_c.astype(F32)
    p_sum = jnp.zeros((tq, n_cmp_pad), F32)
    o_cmp = []
    for pair in range(n_pair):
        o_pair = None
        for hh in range(2):
            h = 2 * pair + hh
            slope = slope_ref[(grp * HPG + h) * SLOPE_ROW + 2 * N_SLOPE_PIECES]
            s = _dot_nt(q_src[h].astype(BF16), k_cmp_x) - slope * dist_cf
            s = jnp.where(valid_c, s, NEG_INF)
            m = jnp.max(s, axis=-1, keepdims=True)
            p = jnp.where(valid_c, jnp.exp2(s - m), 0.0)
            l = jnp.sum(p, axis=-1, keepdims=True)
            p = p * jnp.where(l > 0.0, 1.0 / l, 0.0)
            p_sum = p_sum + p
            d = _dot(p.astype(BF16), v_cmp_x[hh])
            o_pair = d if o_pair is None else o_pair + d
        o_cmp.append(o_pair)

    j_ov = lax.broadcasted_iota(jnp.int32, (n_sel, n_cmp_pad), 0) * SEL_BLOCK
    c_ov = lax.broadcasted_iota(jnp.int32, (n_sel, n_cmp_pad), 1) * CMP_STRIDE
    overlap_t = jnp.where(c_ov < j_ov + SEL_BLOCK, jnp.where(c_ov + CMP_BLOCK > j_ov, 1.0, 0.0), 0.0).astype(BF16)
    p_hi = p_sum.astype(BF16)
    p_lo = (p_sum - p_hi.astype(F32)).astype(BF16)
    imp = _dot_nt(overlap_t, p_hi) + _dot_nt(overlap_t, p_lo)
    j_blk = lax.broadcasted_iota(jnp.int32, (n_sel, tq), 0)
    cur = _div_pow2(q0 + lax.broadcasted_iota(jnp.int32, (n_sel, tq), 1), SEL_BLOCK)
    forced = (j_blk == 0) | (j_blk == cur) | (j_blk == cur - 1)
    imp = jnp.where(forced, FORCE_SCORE, jnp.where(j_blk > cur, NEG_INF, imp))
    n_tile = n_sel // SUBLANES
    tiles = [imp[v * SUBLANES:(v + 1) * SUBLANES, :] for v in range(n_tile)]
    j_loc = lax.broadcasted_iota(jnp.int32, (SUBLANES, tq), 0)
    rank = [jnp.zeros((SUBLANES, tq), F32) for _ in range(n_tile)]
    for i in range(n_sel):
        v, i_loc = divmod(i, SUBLANES)
        row = tiles[v][i_loc:i_loc + 1, :]
        for w in range(n_tile):
            if w > v:
                ahead = jnp.where(row >= tiles[w], 1.0, 0.0)
            elif w < v:
                ahead = jnp.where(row > tiles[w], 1.0, 0.0)
            else:
                ahead = jnp.where(j_loc > i_loc, jnp.where(row >= tiles[w], 1.0, 0.0),
                                  jnp.where(row > tiles[w], 1.0, 0.0))
            rank[w] = rank[w] + ahead
    top_n = float(min(SEL_TOPN, n_sel))
    assert EXT_LANES - EXT_ONEHOT == n_sel
    mask_rows = [jnp.zeros((HEAD_DIM + EXT_ONEHOT, tq), F32)] + [jnp.where(r < top_n, 0.0, NEG_INF) for r in rank]
    mask_ext = jnp.transpose(jnp.concatenate(mask_rows, axis=0))

    q_aug, q_win = [], []
    for h in range(HPG):
        base = (grp * HPG + h) * SLOPE_ROW
        slope_row = jnp.zeros((1, LANES), F32)
        for j in range(2 * N_SLOPE_PIECES):
            slope_row = jnp.where(lane == HEAD_DIM + j, slope_ref[base + j], slope_row)
        q_aug.append(jnp.where(low, q_src[h], mask_ext + slope_row).astype(BF16))
        q_win.append(jnp.where(low, q_src[h], slope_row).astype(BF16))

    def tile_lanes(x, n):
        return jnp.concatenate([x] * n, axis=1)

    def chunk_init(queries, which_k, v_base, k0, bias):
        k_c = kx_ref[0, which_k, 0, pl.ds(k0, kb), :]
        ms, accs = [], []
        for pair in range(n_pair):
            pv = None
            for hh in range(2):
                s = _dot_nt(queries[2 * pair + hh], k_c) + bias
                m = jnp.max(s, axis=-1, keepdims=True)
                p = jnp.exp2(s - m).astype(BF16)
                d = _dot(p, vx_ref[0, v_base + hh, 0, pl.ds(k0, kb), :])
                pv = d if pv is None else pv + d
                ms.append(m)
            accs.append(pv)
        return ms, accs

    assert kb - tq == WINDOW
    k0 = pl.multiple_of(jnp.maximum(q0 - WINDOW, 0), LANES)
    pos = k0 + lax.broadcasted_iota(jnp.int32, (1, kb), 1)
    n_loop = _div_pow2(step * tq, kb)
    causal = pos <= t_col
    sel_bias = jnp.where(causal, jnp.where(pos >= n_loop * kb, 0.0, NEG_INF), NEG_INF)
    win_bias = jnp.where(causal, jnp.where(pos > t_col - WINDOW, 0.0, NEG_INF), NEG_INF)

    _, acc_win = chunk_init(q_win, 1, 2, k0, win_bias)

    m_sel, acc_sel = chunk_init(q_aug, 0, 0, k0, sel_bias)
    for h in range(HPG):
        m_ref[h] = jnp.broadcast_to(m_sel[h], (tq, LANES))
    for pair in range(n_pair):
        acc_ref[pair] = acc_sel[pair]

    def sel_step(c, carry):
        kc0 = pl.multiple_of(c * kb, kb)
        k_c = kx_ref[0, 0, 0, pl.ds(kc0, kb), :]
        for pair in range(n_pair):
            pv, scale = None, []
            for hh in range(2):
                h = 2 * pair + hh
                s = _dot_nt(q_aug[h], k_c)
                m_old = m_ref[h]
                m_new = jnp.maximum(m_old, jnp.max(s, axis=-1, keepdims=True))
                m_ref[h] = m_new
                scale.append(tile_lanes(jnp.exp2(m_old - m_new), 2))
                p = jnp.exp2(s - tile_lanes(m_new, kb // LANES)).astype(BF16)
                d = _dot(p, vx_ref[0, hh, 0, pl.ds(kc0, kb), :])
                pv = d if pv is None else pv + d
            acc_ref[pair] = jnp.where(first_head, scale[0], scale[1]) * acc_ref[pair] + pv
        return carry

    lax.fori_loop(0, n_loop, sel_step, 0)

    gates = gate_ref[0]
    g_hi = gates.astype(BF16)
    g_lo = (gates - g_hi.astype(F32)).astype(BF16)
    g_exp = _dot(jnp.concatenate([g_hi, g_lo], axis=1), expand_ref[0])
    outs = []
    for pair in range(n_pair):
        a_sel = acc_ref[pair]
        a_win = acc_win[pair]
        lo = pair * LANES
        width = HPG * HEAD_DIM
        outs.append(g_exp[:, lo:lo + LANES] * o_cmp[pair]
                    + g_exp[:, width + lo:width + lo + LANES] * (a_sel[:, :LANES] / a_sel[:, LANES:])
                    + g_exp[:, 2 * width + lo:2 * width + lo + LANES] * (a_win[:, :LANES] / a_win[:, LANES:]))
    o_ref[0] = jnp.concatenate(outs, axis=1)


def _gate_expansion():
    r = jnp.arange(2 * GATE_LANES)[None, :, None] % GATE_LANES
    c = jnp.arange(N_BRANCH * HPG * HEAD_DIM)[None, None, :]
    g = jnp.arange(N_KV)[:, None, None]
    br, h = c // (HPG * HEAD_DIM), (c // HEAD_DIM) % HPG
    return (r == g * HPG * N_BRANCH + h * N_BRANCH + br).astype(BF16)


def _attention(slope_table, q, cmp_kv, kx, vx, gates):
    B, S, _ = q.shape
    tq = ATT_TQ
    n_cmp_pad = cmp_kv.shape[3]
    expand = _gate_expansion()
    grid_spec = pltpu.PrefetchScalarGridSpec(
        num_scalar_prefetch=1,
        grid=(B, N_KV, S // tq),
        in_specs=[
            pl.BlockSpec((1, tq, HPG * HEAD_DIM), lambda b, g, i, sl: (b, i, g)),
            pl.BlockSpec((1, 2, 1, n_cmp_pad, HEAD_DIM), lambda b, g, i, sl: (b, 0, g, 0, 0)),
            pl.BlockSpec((1, 2, 1, S, LANES), lambda b, g, i, sl: (b, 0, g, 0, 0)),
            pl.BlockSpec((1, 4, 1, S, 2 * LANES), lambda b, g, i, sl: (b, 0, g, 0, 0)),
            pl.BlockSpec((1, tq, GATE_LANES), lambda b, g, i, sl: (b, i, 0)),
            pl.BlockSpec((1,) + expand.shape[1:], lambda b, g, i, sl: (g, 0, 0)),
        ],
        out_specs=pl.BlockSpec((1, tq, HPG * HEAD_DIM), lambda b, g, i, sl: (b, i, g)),
        scratch_shapes=[
            pltpu.VMEM((HPG, tq, LANES), F32),
            pltpu.VMEM((HPG // 2, tq, 2 * LANES), F32),
        ],
    )
    return pl.pallas_call(
        _attn_kernel,
        grid_spec=grid_spec,
        out_shape=jax.ShapeDtypeStruct((B, S, D_ATT), F32),
        compiler_params=pltpu.CompilerParams(
            dimension_semantics=("parallel", "parallel", "arbitrary"),
            vmem_limit_bytes=VMEM_LIMIT_BYTES),
        name="nsa_attention",
    )(slope_table, q, cmp_kv, kx, vx, gates, expand)


def _attn_kernel_old(slope_ref, q_ref, cmp_ref, kvs_ref, gate_ref, o_ref):
    tq = q_ref.shape[1]
    n_cmp_pad = cmp_ref.shape[3]
    seq = kvs_ref.shape[3]
    n_sel = seq // SEL_BLOCK
    kb = tq
    grp = pl.program_id(1)
    step = pl.program_id(2)
    q0 = step * tq

    qf = q_ref[0].astype(F32)
    qh = [qf[:, h * HEAD_DIM:(h + 1) * HEAD_DIM] for h in range(HPG)]
    t_col = q0 + lax.broadcasted_iota(jnp.int32, (tq, 1), 0)

    k_cmp = cmp_ref[0, 0, 0]
    v_cmp = cmp_ref[0, 1, 0]
    c_row = lax.broadcasted_iota(jnp.int32, (1, n_cmp_pad), 1)
    dist_c = t_col - (c_row * CMP_STRIDE + (CMP_BLOCK - 1))
    valid_c = dist_c >= 0
    dist_cf = dist_c.astype(F32)
    p_sum = jnp.zeros((tq, n_cmp_pad), F32)
    o_cmp = []
    for h in range(HPG):
        slope = slope_ref[(grp * HPG + h) * SLOPE_ROW + 2 * N_SLOPE_PIECES]
        s = _dot_nt(qh[h].astype(BF16), k_cmp) - slope * dist_cf
        s = jnp.where(valid_c, s, NEG_INF)
        m = jnp.max(s, axis=-1, keepdims=True)
        p = jnp.where(valid_c, jnp.exp2(s - m), 0.0)
        l = jnp.sum(p, axis=-1, keepdims=True)
        p = p * jnp.where(l > 0.0, 1.0 / l, 0.0)
        p_sum = p_sum + p
        o_cmp.append(_dot(p.astype(BF16), v_cmp))

    j_ov = lax.broadcasted_iota(jnp.int32, (n_sel, n_cmp_pad), 0) * SEL_BLOCK
    c_ov = lax.broadcasted_iota(jnp.int32, (n_sel, n_cmp_pad), 1) * CMP_STRIDE
    overlap_t = jnp.where((c_ov < j_ov + SEL_BLOCK) & (c_ov + CMP_BLOCK > j_ov), 1.0, 0.0).astype(BF16)
    p_hi = p_sum.astype(BF16)
    p_lo = (p_sum - p_hi.astype(F32)).astype(BF16)
    imp = _dot_nt(overlap_t, p_hi) + _dot_nt(overlap_t, p_lo)
    j_blk = lax.broadcasted_iota(jnp.int32, (n_sel, tq), 0)
    cur = _div_pow2(q0 + lax.broadcasted_iota(jnp.int32, (n_sel, tq), 1), SEL_BLOCK)
    forced = (j_blk == 0) | (j_blk == cur) | (j_blk == cur - 1)
    imp = jnp.where(forced, FORCE_SCORE, jnp.where(j_blk > cur, NEG_INF, imp))
    rank = jnp.zeros((n_sel, tq), jnp.int32)
    for i in range(n_sel):
        row = imp[i:i + 1, :]
        rank = rank + jnp.where(j_blk > i, jnp.where(row >= imp, 1, 0), jnp.where(row > imp, 1, 0))
    mask_t = jnp.where(rank < min(SEL_TOPN, n_sel), 0.0, NEG_INF)
    assert EXT_LANES - EXT_ONEHOT == n_sel
    mask_ext = jnp.transpose(jnp.concatenate([jnp.zeros((EXT_ONEHOT, tq), F32), mask_t], axis=0))

    lane_ext = lax.broadcasted_iota(jnp.int32, (1, EXT_LANES), 1)
    q_aug = []
    for h in range(HPG):
        base = (grp * HPG + h) * SLOPE_ROW
        slope_ext = jnp.zeros((1, EXT_LANES), F32)
        for j in range(2 * N_SLOPE_PIECES):
            slope_ext = jnp.where(lane_ext == j, slope_ref[base + j], slope_ext)
        q_aug.append(jnp.concatenate([qh[h], mask_ext + slope_ext], axis=-1).astype(BF16))
    q_all = jnp.concatenate(q_aug, axis=0)

    def add_bias(s, bias):
        return (s.reshape(HPG, tq, -1) + bias[None]).reshape(HPG * tq, -1)

    def sel_update(carry, k0, bias):
        m_old, acc_old = carry
        k_c = kvs_ref[0, 0, 0, pl.ds(k0, kb), :]
        v_c = kvs_ref[0, 1, 0, pl.ds(k0, kb), :]
        s = _dot_nt(q_all, k_c)
        if bias is not None:
            s = add_bias(s, bias)
        m_new = jnp.maximum(m_old, jnp.max(s, axis=-1, keepdims=True))
        p = jnp.exp2(s - m_new)
        acc_new = jnp.exp2(m_old - m_new) * acc_old + _dot(p.astype(BF16), v_c)
        return m_new, acc_new

    init = (jnp.full((HPG * tq, 1), NEG_INF, F32), jnp.zeros((HPG * tq, LANES), F32))
    past = lax.fori_loop(0, step, lambda c, carry: sel_update(carry, pl.multiple_of(c * kb, kb), None), init)
    r_col = lax.broadcasted_iota(jnp.int32, (tq, kb), 0)
    r_row = lax.broadcasted_iota(jnp.int32, (tq, kb), 1)
    causal_bias = jnp.where(r_row <= r_col, 0.0, NEG_INF)
    _, acc_sel = sel_update(past, pl.multiple_of(q0, kb), causal_bias)

    n_win = WINDOW + tq
    w0 = pl.multiple_of(jnp.maximum(q0 - WINDOW, 0), LANES)
    k_w = kvs_ref[0, 2, 0, pl.ds(w0, n_win), :]
    v_w = kvs_ref[0, 3, 0, pl.ds(w0, n_win), :]
    dist_w = t_col - (w0 + lax.broadcasted_iota(jnp.int32, (1, n_win), 1))
    win_bias = jnp.where((dist_w >= 0) & (dist_w < WINDOW), 0.0, NEG_INF)
    s = add_bias(_dot_nt(q_all, k_w), win_bias)
    p = jnp.exp2(s - jnp.max(s, axis=-1, keepdims=True))
    acc_win = _dot(p.astype(BF16), v_w)

    gates = gate_ref[0, 0]
    outs = []
    for h in range(HPG):
        rows = slice(h * tq, (h + 1) * tq)
        o_sel = acc_sel[rows, :HEAD_DIM] / acc_sel[rows, HEAD_DIM:HEAD_DIM + 1]
        o_win = acc_win[rows, :HEAD_DIM] / acc_win[rows, HEAD_DIM:HEAD_DIM + 1]
        gb = gates[:, h * N_BRANCH:(h + 1) * N_BRANCH]
        outs.append(gb[:, 0:1] * o_cmp[h] + gb[:, 1:2] * o_sel + gb[:, 2:3] * o_win)
    o_ref[0] = jnp.concatenate(outs, axis=-1)


def _attention_old(slope_table, q, cmp_kv, kvs, gates):
    B, S, _ = q.shape
    tq = ATT_TQ
    n_cmp_pad = cmp_kv.shape[3]
    per_group = HPG * N_BRANCH
    grid_spec = pltpu.PrefetchScalarGridSpec(
        num_scalar_prefetch=1,
        grid=(B, N_KV, S // tq),
        in_specs=[
            pl.BlockSpec((1, tq, HPG * HEAD_DIM), lambda b, g, i, sl: (b, i, g)),
            pl.BlockSpec((1, 2, 1, n_cmp_pad, HEAD_DIM), lambda b, g, i, sl: (b, 0, g, 0, 0)),
            pl.BlockSpec((1, 4, 1, S, LANES), lambda b, g, i, sl: (b, 0, g, 0, 0)),
            pl.BlockSpec((1, 1, tq, per_group), lambda b, g, i, sl: (b, g, i, 0)),
        ],
        out_specs=pl.BlockSpec((1, tq, HPG * HEAD_DIM), lambda b, g, i, sl: (b, i, g)),
    )
    return pl.pallas_call(
        _attn_kernel,
        grid_spec=grid_spec,
        out_shape=jax.ShapeDtypeStruct((B, S, D_ATT), F32),
        compiler_params=pltpu.CompilerParams(
            dimension_semantics=("parallel", "parallel", "arbitrary"),
            vmem_limit_bytes=VMEM_LIMIT_BYTES),
        name="nsa_attention",
    )(slope_table, q, cmp_kv, kvs, gates)


def _loop_chunks(tile, tq, kb):
    return (tile * tq) // kb


def _attn_pair_kernel(slope_ref, qa_ref, qb_ref, cmp_ref, kx_ref, vx_ref, ga_ref, gb_ref, za_ref, zb_ref, expand_ref,
                      oa_ref, ob_ref, qs_ref, m_ref, acc_ref):
    tq = qa_ref.shape[1]
    n_cmp_pad = cmp_ref.shape[3]
    seq = kx_ref.shape[3]
    n_sel = seq // SEL_BLOCK
    n_tiles = seq // tq
    kb = ATT_KB
    n_pair = HPG // 2
    grp = pl.program_id(1)
    pstep = pl.program_id(2)
    assert kb - tq == WINDOW and EXT_LANES - EXT_ONEHOT == n_sel
    n_static = _loop_chunks(0, tq, kb) + _loop_chunks(n_tiles - 1, tq, kb)
    assert all(_loop_chunks(i, tq, kb) + _loop_chunks(n_tiles - 1 - i, tq, kb) == n_static for i in range(n_tiles // 2))

    lane = lax.broadcasted_iota(jnp.int32, (1, LANES), 1)
    low = lane < HEAD_DIM
    first_head = jnp.bitwise_and(lax.broadcasted_iota(jnp.int32, (1, 2 * LANES), 1), HEAD_DIM) == 0

    k_cmp = cmp_ref[0, 0, 0].astype(F32)
    v_cmp = cmp_ref[0, 1, 0].astype(F32)
    zero_half = jnp.zeros_like(v_cmp)
    k_cmp_x = jnp.concatenate([k_cmp, zero_half], axis=1).astype(BF16)
    v_cmp_x = (jnp.concatenate([v_cmp, zero_half], axis=1).astype(BF16),
               jnp.concatenate([zero_half, v_cmp], axis=1).astype(BF16))
    j_ov = lax.broadcasted_iota(jnp.int32, (n_sel, n_cmp_pad), 0) * SEL_BLOCK
    c_ov = lax.broadcasted_iota(jnp.int32, (n_sel, n_cmp_pad), 1) * CMP_STRIDE
    overlap_t = jnp.where(c_ov < j_ov + SEL_BLOCK, jnp.where(c_ov + CMP_BLOCK > j_ov, 1.0, 0.0), 0.0).astype(BF16)
    slope_rows, slopes = [], []
    for h in range(HPG):
        base = (grp * HPG + h) * SLOPE_ROW
        row = jnp.zeros((1, LANES), F32)
        for j in range(2 * N_SLOPE_PIECES):
            row = jnp.where(lane == HEAD_DIM + j, slope_ref[base + j], row)
        slope_rows.append(row)
        slopes.append(slope_ref[base + 2 * N_SLOPE_PIECES])
    c_row = lax.broadcasted_iota(jnp.int32, (1, n_cmp_pad), 1)
    j_blk = lax.broadcasted_iota(jnp.int32, (n_sel, tq), 0)
    j_loc = lax.broadcasted_iota(jnp.int32, (SUBLANES, tq), 0)
    n_tile = n_sel // SUBLANES
    top_n = float(min(SEL_TOPN, n_sel))

    def tile_lanes(x, n):
        return jnp.concatenate([x] * n, axis=1)

    def prepare(q_ref, q0):
        t_col = q0 + lax.broadcasted_iota(jnp.int32, (tq, 1), 0)
        qf = q_ref[0].astype(F32)
        q_src = []
        for pair in range(n_pair):
            both = qf[:, pair * LANES:(pair + 1) * LANES]
            q_src += [both, pltpu.roll(both, HEAD_DIM, axis=1)]
        q_win = [jnp.where(low, q_src[h], slope_rows[h]).astype(BF16) for h in range(HPG)]

        dist_c = t_col - (c_row * CMP_STRIDE + (CMP_BLOCK - 1))
        valid_c = dist_c >= 0
        dist_cf = dist_c.astype(F32)
        p_sum = jnp.zeros((tq, n_cmp_pad), F32)
        o_cmp = []
        for pair in range(n_pair):
            o_pair = None
            for hh in range(2):
                h = 2 * pair + hh
                s = _dot_nt(q_src[h].astype(BF16), k_cmp_x) - slopes[h] * dist_cf
                s = jnp.where(valid_c, s, NEG_INF)
                m = jnp.max(s, axis=-1, keepdims=True)
                p = jnp.where(valid_c, jnp.exp2(s - m), 0.0)
                l = jnp.sum(p, axis=-1, keepdims=True)
                p = p * jnp.where(l > 0.0, 1.0 / l, 0.0)
                p_sum = p_sum + p
                d = _dot(p.astype(BF16), v_cmp_x[hh])
                o_pair = d if o_pair is None else o_pair + d
            o_cmp.append(o_pair)

        p_hi = p_sum.astype(BF16)
        p_lo = (p_sum - p_hi.astype(F32)).astype(BF16)
        imp = _dot_nt(overlap_t, p_hi) + _dot_nt(overlap_t, p_lo)
        cur = _div_pow2(q0 + lax.broadcasted_iota(jnp.int32, (n_sel, tq), 1), SEL_BLOCK)
        forced = (j_blk == 0) | (j_blk == cur) | (j_blk == cur - 1)
        imp = jnp.where(forced, FORCE_SCORE, jnp.where(j_blk > cur, NEG_INF, imp))
        tiles = [imp[v * SUBLANES:(v + 1) * SUBLANES, :] for v in range(n_tile)]
        rank = [jnp.zeros((SUBLANES, tq), F32) for _ in range(n_tile)]
        for i in range(n_sel):
            v, i_loc = divmod(i, SUBLANES)
            row = tiles[v][i_loc:i_loc + 1, :]
            for w in range(n_tile):
                if w > v:
                    ahead = jnp.where(row >= tiles[w], 1.0, 0.0)
                elif w < v:
                    ahead = jnp.where(row > tiles[w], 1.0, 0.0)
                else:
                    ahead = jnp.where(j_loc > i_loc, jnp.where(row >= tiles[w], 1.0, 0.0),
                                      jnp.where(row > tiles[w], 1.0, 0.0))
                rank[w] = rank[w] + ahead
        mask_rows = [jnp.zeros((HEAD_DIM + EXT_ONEHOT, tq), F32)] + [jnp.where(r < top_n, 0.0, NEG_INF) for r in rank]
        mask_ext = jnp.transpose(jnp.concatenate(mask_rows, axis=0))
        q_aug = [jnp.where(low, q_src[h], mask_ext + slope_rows[h]).astype(BF16) for h in range(HPG)]
        return t_col, o_cmp, q_aug, q_win

    def chunk_init(queries, which_k, v_base, k0, bias):
        k_c = kx_ref[0, which_k, 0, pl.ds(k0, kb), :]
        ms, accs = [], []
        for pair in range(n_pair):
            pv = None
            for hh in range(2):
                s = _dot_nt(queries[2 * pair + hh], k_c) + bias
                m = jnp.max(s, axis=-1, keepdims=True)
                p = jnp.exp2(s - m).astype(BF16)
                d = _dot(p, vx_ref[0, v_base + hh, 0, pl.ds(k0, kb), :])
                pv = d if pv is None else pv + d
                ms.append(m)
            accs.append(pv)
        return ms, accs

    tile_of = (pstep, n_tiles - 1 - pstep)
    q_refs = (qa_ref, qb_ref)
    n_loop, o_cmp, acc_win = [], [], []
    for t in range(2):
        q0 = tile_of[t] * tq
        t_col, o_cmp_t, q_aug, q_win = prepare(q_refs[t], q0)
        o_cmp.append(o_cmp_t)
        k0 = pl.multiple_of(jnp.maximum(q0 - WINDOW, 0), LANES)
        pos = k0 + lax.broadcasted_iota(jnp.int32, (1, kb), 1)
        n_loop.append(_div_pow2(q0, kb))
        causal = pos <= t_col
        sel_bias = jnp.where(causal, jnp.where(pos >= n_loop[t] * kb, 0.0, NEG_INF), NEG_INF)
        win_bias = jnp.where(causal, jnp.where(pos > t_col - WINDOW, 0.0, NEG_INF), NEG_INF)
        acc_win.append(chunk_init(q_win, 1, 2, k0, win_bias)[1])
        m_sel, acc_sel = chunk_init(q_aug, 0, 0, k0, sel_bias)
        for h in range(HPG):
            m_ref[t, h] = jnp.broadcast_to(m_sel[h], (tq, LANES))
            qs_ref[t, h] = q_aug[h]
        for pair in range(n_pair):
            acc_ref[t, pair] = acc_sel[pair]

    for j in range(n_static):
        mine = j < n_loop[0]
        t = jnp.where(mine, 0, 1)
        kc0 = pl.multiple_of(jnp.where(mine, j, j - n_loop[0]) * kb, kb)
        k_c = kx_ref[0, 0, 0, pl.ds(kc0, kb), :]
        for pair in range(n_pair):
            pv, scale = None, []
            for hh in range(2):
                h = 2 * pair + hh
                s = _dot_nt(qs_ref[t, h], k_c)
                m_old = m_ref[t, h]
                m_new = jnp.maximum(m_old, jnp.max(s, axis=-1, keepdims=True))
                m_ref[t, h] = m_new
                scale.append(tile_lanes(jnp.exp2(m_old - m_new), 2))
                p = jnp.exp2(s - tile_lanes(m_new, kb // LANES)).astype(BF16)
                d = _dot(p, vx_ref[0, hh, 0, pl.ds(kc0, kb), :])
                pv = d if pv is None else pv + d
            acc_ref[t, pair] = jnp.where(first_head, scale[0], scale[1]) * acc_ref[t, pair] + pv

    width = HPG * HEAD_DIM
    for t, (g_ref, z_ref, o_ref) in enumerate(((ga_ref, za_ref, oa_ref), (gb_ref, zb_ref, ob_ref))):
        gates = g_ref[0]
        g_hi = gates.astype(BF16)
        g_lo = (gates - g_hi.astype(F32)).astype(BF16)
        g_exp = _dot(jnp.concatenate([g_hi, g_lo], axis=1), expand_ref[0])
        outs = []
        for pair in range(n_pair):
            a_sel = acc_ref[t, pair]
            a_win = acc_win[t][pair]
            lo = pair * LANES
            outs.append(g_exp[:, lo:lo + LANES] * o_cmp[t][pair]
                        + g_exp[:, width + lo:width + lo + LANES] * (a_sel[:, :LANES] / a_sel[:, LANES:])
                        + g_exp[:, 2 * width + lo:2 * width + lo + LANES] * (a_win[:, :LANES] / a_win[:, LANES:]))
        o_ref[0] = (jnp.concatenate(outs, axis=1) * _silu(z_ref[0])).astype(BF16)


def _attention_pairs(slope_table, q, cmp_kv, kx, vx, gates, z):
    B, S, _ = q.shape
    tq = ATT_TQ
    n_tiles = S // tq
    n_half = n_tiles // 2
    n_cmp_pad = cmp_kv.shape[3]
    expand = _gate_expansion()
    width = HPG * HEAD_DIM
    grid_spec = pltpu.PrefetchScalarGridSpec(
        num_scalar_prefetch=1,
        grid=(B, N_KV, n_half),
        in_specs=[
            pl.BlockSpec((1, tq, width), lambda b, g, i, sl: (b, i, g)),
            pl.BlockSpec((1, tq, width), lambda b, g, i, sl: (b, n_tiles - 1 - i, g)),
            pl.BlockSpec((1, 2, 1, n_cmp_pad, HEAD_DIM), lambda b, g, i, sl: (b, 0, g, 0, 0)),
            pl.BlockSpec((1, 2, 1, S, LANES), lambda b, g, i, sl: (b, 0, g, 0, 0)),
            pl.BlockSpec((1, 4, 1, S, 2 * LANES), lambda b, g, i, sl: (b, 0, g, 0, 0)),
            pl.BlockSpec((1, tq, GATE_LANES), lambda b, g, i, sl: (b, i, 0)),
            pl.BlockSpec((1, tq, GATE_LANES), lambda b, g, i, sl: (b, n_tiles - 1 - i, 0)),
            pl.BlockSpec((1, tq, width), lambda b, g, i, sl: (b, i, g)),
            pl.BlockSpec((1, tq, width), lambda b, g, i, sl: (b, n_tiles - 1 - i, g)),
            pl.BlockSpec((1,) + expand.shape[1:], lambda b, g, i, sl: (g, 0, 0)),
        ],
        out_specs=[
            pl.BlockSpec((1, tq, width), lambda b, g, i, sl: (b, i, g)),
            pl.BlockSpec((1, tq, width), lambda b, g, i, sl: (b, n_half - 1 - i, g)),
        ],
        scratch_shapes=[
            pltpu.VMEM((2, HPG, tq, LANES), BF16),
            pltpu.VMEM((2, HPG, tq, LANES), F32),
            pltpu.VMEM((2, HPG // 2, tq, 2 * LANES), F32),
        ],
    )
    half = jax.ShapeDtypeStruct((B, S // 2, D_ATT), BF16)
    return pl.pallas_call(
        _attn_pair_kernel,
        grid_spec=grid_spec,
        out_shape=[half, half],
        compiler_params=pltpu.CompilerParams(
            dimension_semantics=("parallel", "parallel", "arbitrary"),
            vmem_limit_bytes=VMEM_LIMIT_BYTES),
        name="nsa_attention",
    )(slope_table, q, q, cmp_kv, kx, vx, gates, gates, z, z, expand)


def _nsa_out_kernel(o_lo_ref, o_hi_ref, x_ref, w_ref, g_ref, b_ref, out_ref):
    first_half = pl.program_id(1) < pl.num_programs(1) // 2
    gated = jnp.where(first_half, o_lo_ref[0], o_hi_ref[0])
    y = _dot(gated, w_ref[...])
    out_ref[0] = _layer_norm(DN_ALPHA * x_ref[0] + y, g_ref[...], b_ref[...])


def _nsa_out(o_lo, o_hi, x, w_out, ln_g, ln_b):
    B, S, D = x.shape
    tm = PROJ_TM
    n_half = S // tm // 2
    tile = pl.BlockSpec((1, tm, D), lambda b, s: (b, s, 0))
    lo_tile = pl.BlockSpec((1, tm, D), lambda b, s: (b, jnp.minimum(s, n_half - 1), 0))
    hi_tile = pl.BlockSpec((1, tm, D), lambda b, s: (b, jnp.maximum(s - n_half, 0), 0))
    return pl.pallas_call(
        _nsa_out_kernel,
        grid=(B, S // tm),
        in_specs=[lo_tile, hi_tile, tile,
                  _const_spec(w_out.shape), _const_spec(ln_g.shape), _const_spec(ln_b.shape)],
        out_specs=tile,
        out_shape=jax.ShapeDtypeStruct((B, S, D), F32),
        compiler_params=pltpu.CompilerParams(
            dimension_semantics=("parallel", "parallel"),
            vmem_limit_bytes=VMEM_LIMIT_BYTES),
        name="nsa_out",
    )(o_lo, o_hi, x, w_out, ln_g, ln_b)


def _slope_table():
    h = jnp.arange(1, N_HEADS + 1, dtype=F32)
    slope = (2.0 ** (-8.0 * h / N_HEADS)) * LOG2E
    pieces, rest = [], slope
    for _ in range(N_SLOPE_PIECES):
        piece = rest.astype(BF16).astype(F32)
        pieces.append(piece)
        rest = rest - piece
    cols = [SEL_BLOCK * p for p in pieces] + pieces + [slope]
    cols += [jnp.zeros_like(slope)] * (SLOPE_ROW - len(cols))
    return jnp.stack(cols, axis=1).reshape(-1)


def kernel(x, ln_g, ln_b, pool_w_in, pool_w_grp, pool_scale, pool_w_out, nsa_w_in, nsa_cmp_pos_k, nsa_cmp_w1_k,
           nsa_cmp_w2_k, nsa_cmp_pos_v, nsa_cmp_w1_v, nsa_cmp_w2_v, nsa_w_out):
    seq = x.shape[1]
    assert seq % max(POOL_TM, PROJ_TM, ATT_TQ) == 0 and x.shape[2] == D_MODEL
    assert seq >= WINDOW + ATT_TQ and WINDOW % ATT_TQ == 0 and ATT_TQ % SEL_BLOCK == 0
    assert seq // SEL_BLOCK == EXT_LANES - EXT_ONEHOT

    x = _pool_layer(x, pool_w_in[0].astype(BF16), pool_w_grp[0].astype(BF16), pool_scale[0][None, :],
                    pool_w_out[0].astype(BF16), ln_g[0][None, :], ln_b[0][None, :])

    w = nsa_w_in[0]
    kv_lo, z_lo, gl_lo = D_ATT, D_ATT + 6 * D_KV, 2 * D_ATT + 6 * D_KV
    wq = w[:, :kv_lo].astype(BF16)
    wkv = w[:, kv_lo:z_lo].astype(BF16)
    wz = w[:, z_lo:gl_lo].astype(BF16)
    wgl = jnp.pad(w[:, gl_lo:], ((0, 0), (0, GATE_LANES - N_BRANCH * N_HEADS))).astype(BF16)
    q, cmp_in, kx, vx, z, gates = _nsa_inproj(x, wq, wkv, wz, wgl)

    half = CMP_BLOCK // 2
    pos = jnp.stack([nsa_cmp_pos_k[0], nsa_cmp_pos_v[0]]).reshape(2, 2, half * HEAD_DIM)
    w1 = jnp.stack([nsa_cmp_w1_k[0], nsa_cmp_w1_v[0]]).astype(BF16)
    w2 = jnp.stack([nsa_cmp_w2_k[0], nsa_cmp_w2_v[0]]).astype(BF16)
    cmp_kv = _compress(cmp_in, pos, w1, w2)

    o_lo, o_hi = _attention_pairs(_slope_table(), q, cmp_kv, kx, vx, gates, z)
    return _nsa_out(o_lo, o_hi, x, nsa_w_out[0].astype(BF16), ln_g[1][None, :], ln_b[1][None, :])
```

```python
import math

import jax
import jax.numpy as jnp
from jax import lax
from jax.experimental import pallas as pl
from jax.experimental.pallas import tpu as pltpu

F32 = jnp.float32
BF16 = jnp.bfloat16

D_MODEL = 1024
DEPTH = 2
D_POOL = 2 * D_MODEL
POOL_WINDOWS = (2, 4, 8, 16)
POOL_GROUP = D_POOL // len(POOL_WINDOWS)
HEAD_DIM = 64
N_HEADS = D_MODEL // HEAD_DIM
N_KV = 4
HPG = N_HEADS // N_KV
D_ATT = N_HEADS * HEAD_DIM
D_KV = N_KV * HEAD_DIM
N_BRANCH = 3
CMP_BLOCK = 32
CMP_STRIDE = 16
SEL_BLOCK = 64
SEL_TOPN = 8
WINDOW = 256
DN_ALPHA = (2.0 * DEPTH) ** 0.25
LN_EPS = 1e-5
NEG_INF = -1e30
FORCE_SCORE = 1e9
LOG2E = math.log2(math.e)

LANES = 128
SUBLANES = 8
VMEM_LIMIT_BYTES = 56 * 1024 * 1024

POOL_TM = 512
POOL_HALO = 32
PROJ_TM = 512
ATT_TQ = 256
ATT_KB = 512
GATE_LANES = LANES

N_SLOPE_PIECES = 3
EXT_ONEHOT = 32
SLOPE_ROW = 8
EXT_LANES = LANES - HEAD_DIM


def _dot(a, b):
    return jnp.dot(a, b, preferred_element_type=F32)


def _dot_nt(a, b):
    return lax.dot_general(a, b, (((1,), (1,)), ((), ())), preferred_element_type=F32)


def _div_pow2(x, n):
    assert n & (n - 1) == 0
    return lax.shift_right_logical(x, n.bit_length() - 1)


def _silu(x):
    return x * jax.nn.sigmoid(x)


def _layer_norm(r, g, b):
    mu = jnp.mean(r, axis=-1, keepdims=True)
    d = r - mu
    var = jnp.mean(d * d, axis=-1, keepdims=True)
    return d * lax.rsqrt(var + LN_EPS) * g + b


def _const_spec(shape):
    n = len(shape)
    return pl.BlockSpec(shape, lambda *_: (0,) * n, pipeline_mode=pl.Buffered(1))


def _pool_layer_kernel(x_ref, w_in_ref, w_grp_ref, scale_ref, w_out_ref, g_ref, b_ref,
                       o_ref, carry_ref, work_a_ref, work_b_ref):
    tm = x_ref.shape[1]
    work = (work_a_ref, work_b_ref)
    s = pl.program_id(1)

    @pl.when(s == 0)
    def _():
        carry_ref[...] = jnp.zeros_like(carry_ref)

    xt = x_ref[0]
    xb = xt.astype(BF16)
    t = s * tm + lax.broadcasted_iota(jnp.int32, (tm, 1), 0)
    y = jnp.zeros((tm, D_MODEL), F32)
    for g, w in enumerate(POOL_WINDOWS):
        lo = g * POOL_GROUP
        u = _dot(xb, w_in_ref[:, lo:lo + POOL_GROUP])
        z = _dot(xb, w_in_ref[:, D_POOL + lo:D_POOL + lo + POOL_GROUP])
        work[0][0:POOL_HALO, :] = carry_ref[g]
        work[0][POOL_HALO:, :] = u
        carry_ref[g] = u[tm - POOL_HALO:, :]
        n_steps = w.bit_length() - 1
        assert w == 1 << n_steps and SUBLANES * n_steps <= POOL_HALO
        src = 0
        for j in range(n_steps - 1):
            shift, lo_row = 1 << j, SUBLANES * (j + 1)
            n = POOL_HALO + tm - lo_row
            work[1 - src][lo_row:, :] = work[src][lo_row:, :] + work[src][lo_row - shift:lo_row - shift + n, :]
            src = 1 - src
        shift = w // 2
        tot = work[src][POOL_HALO:, :] + work[src][POOL_HALO - shift:POOL_HALO - shift + tm, :]
        inv_cnt = 1.0 / jnp.minimum(t + 1, w).astype(F32)
        pooled = tot * inv_cnt - u
        m = _dot(pooled.astype(BF16), w_grp_ref[g]) * scale_ref[:, lo:lo + POOL_GROUP]
        gated = m * _silu(z)
        y = y + _dot(gated.astype(BF16), w_out_ref[lo:lo + POOL_GROUP, :])
    o_ref[0] = _layer_norm(DN_ALPHA * xt + y, g_ref[...], b_ref[...])


def _pool_layer(x, w_in, w_grp, scale, w_out, ln_g, ln_b):
    B, S, D = x.shape
    tm = POOL_TM
    return pl.pallas_call(
        _pool_layer_kernel,
        grid=(B, S // tm),
        in_specs=[
            pl.BlockSpec((1, tm, D), lambda b, s: (b, s, 0)),
            _const_spec(w_in.shape),
            _const_spec(w_grp.shape),
            _const_spec(scale.shape),
            _const_spec(w_out.shape),
            _const_spec(ln_g.shape),
            _const_spec(ln_b.shape),
        ],
        out_specs=pl.BlockSpec((1, tm, D), lambda b, s: (b, s, 0)),
        out_shape=jax.ShapeDtypeStruct((B, S, D), F32),
        scratch_shapes=[
            pltpu.VMEM((len(POOL_WINDOWS), POOL_HALO, POOL_GROUP), F32),
            pltpu.VMEM((POOL_HALO + tm, POOL_GROUP), F32),
            pltpu.VMEM((POOL_HALO + tm, POOL_GROUP), F32),
        ],
        compiler_params=pltpu.CompilerParams(
            dimension_semantics=("parallel", "arbitrary"),
            vmem_limit_bytes=VMEM_LIMIT_BYTES),
        name="pool_layer",
    )(x, w_in, w_grp, scale, w_out, ln_g, ln_b)


def _nsa_inproj_kernel(x_ref, wq_ref, wkv_ref, wz_ref, wgl_ref,
                       q_ref, cmp_ref, kx_ref, vx_ref, z_ref, gate_ref):
    tm = x_ref.shape[1]
    xb = x_ref[0].astype(BF16)
    q_ref[0] = (_dot(xb, wq_ref[...]) * (HEAD_DIM ** -0.5 * LOG2E)).astype(BF16)
    kv = _dot(xb, wkv_ref[...])
    for a in range(2):
        for g in range(N_KV):
            lo = a * D_KV + g * HEAD_DIM
            cmp_ref[0, a, g] = kv[:, lo:lo + HEAD_DIM]

    lane = lax.broadcasted_iota(jnp.int32, (tm, LANES), 1)
    ext = lane - HEAD_DIM
    low = ext < 0
    pos = pl.program_id(1) * tm + lax.broadcasted_iota(jnp.int32, (tm, LANES), 0)
    blk = _div_pow2(pos, SEL_BLOCK)
    alibi = jnp.where(ext < N_SLOPE_PIECES, blk, jnp.where(ext < 2 * N_SLOPE_PIECES, pos - blk * SEL_BLOCK, 0))
    k_win_ext = alibi.astype(F32)
    k_sel_ext = jnp.where(ext >= EXT_ONEHOT, jnp.where(ext - EXT_ONEHOT == blk, 1.0, 0.0), k_win_ext)
    ones_a = jnp.where(low, 1.0, 0.0).astype(BF16)
    ones_b = jnp.where(low, 0.0, 1.0).astype(BF16)
    for a in range(2, 6):
        for pair in range(N_KV // 2):
            lo = a * D_KV + pair * LANES
            both = kv[:, lo:lo + LANES]
            swapped = pltpu.roll(both, HEAD_DIM, axis=1)
            for g, (at_low, at_high) in ((2 * pair, (both, swapped)), (2 * pair + 1, (swapped, both))):
                if a % 2 == 0:
                    k_ext = k_sel_ext if a == 2 else k_win_ext
                    kx_ref[0, (a - 2) // 2, g] = jnp.where(low, at_low, k_ext).astype(BF16)
                else:
                    va = jnp.where(low, at_low, 0.0).astype(BF16)
                    vb = jnp.where(low, 0.0, at_high).astype(BF16)
                    vx_ref[0, a - 3, g] = jnp.concatenate([va, ones_a], axis=1)
                    vx_ref[0, a - 2, g] = jnp.concatenate([vb, ones_b], axis=1)

    z_ref[0] = _dot(xb, wz_ref[...])
    gate_ref[0] = jax.nn.sigmoid(_dot(xb, wgl_ref[...]))


def _nsa_inproj(x, wq, wkv, wz, wgl):
    B, S, D = x.shape
    tm = PROJ_TM
    return pl.pallas_call(
        _nsa_inproj_kernel,
        grid=(B, S // tm),
        in_specs=[
            pl.BlockSpec((1, tm, D), lambda b, s: (b, s, 0)),
            _const_spec(wq.shape),
            _const_spec(wkv.shape),
            _const_spec(wz.shape),
            _const_spec(wgl.shape),
        ],
        out_specs=[
            pl.BlockSpec((1, tm, D_ATT), lambda b, s: (b, s, 0)),
            pl.BlockSpec((1, 2, N_KV, tm, HEAD_DIM), lambda b, s: (b, 0, 0, s, 0)),
            pl.BlockSpec((1, 2, N_KV, tm, LANES), lambda b, s: (b, 0, 0, s, 0)),
            pl.BlockSpec((1, 4, N_KV, tm, 2 * LANES), lambda b, s: (b, 0, 0, s, 0)),
            pl.BlockSpec((1, tm, D_ATT), lambda b, s: (b, s, 0)),
            pl.BlockSpec((1, tm, GATE_LANES), lambda b, s: (b, s, 0)),
        ],
        out_shape=[
            jax.ShapeDtypeStruct((B, S, D_ATT), BF16),
            jax.ShapeDtypeStruct((B, 2, N_KV, S, HEAD_DIM), F32),
            jax.ShapeDtypeStruct((B, 2, N_KV, S, LANES), BF16),
            jax.ShapeDtypeStruct((B, 4, N_KV, S, 2 * LANES), BF16),
            jax.ShapeDtypeStruct((B, S, D_ATT), F32),
            jax.ShapeDtypeStruct((B, S, GATE_LANES), F32),
        ],
        compiler_params=pltpu.CompilerParams(
            dimension_semantics=("parallel", "parallel"),
            vmem_limit_bytes=VMEM_LIMIT_BYTES),
        name="nsa_inproj",
    )(x, wq, wkv, wz, wgl)


def _compress_kernel(c_ref, pos_ref, w1_ref, w2_ref, o_ref):
    n_grp, seq = c_ref.shape[2:4]
    n_row = seq // CMP_STRIDE
    width = CMP_STRIDE * HEAD_DIM
    groups = []
    for g in range(n_grp):
        pieces = [c_ref[0, 0, g, pl.ds(l, n_row, stride=CMP_STRIDE), :] for l in range(CMP_STRIDE)]
        groups.append(jnp.concatenate(pieces, axis=1))
    rows = jnp.concatenate(groups, axis=0)
    nxt = pltpu.roll(rows, shift=n_grp * n_row - 1, axis=0)
    first = (rows + pos_ref[0, 0:1, :]).astype(BF16)
    second = (nxt + pos_ref[0, 1:2, :]).astype(BF16)
    h = _dot(first, w1_ref[0, :width, :]) + _dot(second, w1_ref[0, width:, :])
    out = _dot(_silu(h).astype(BF16), w2_ref[0])
    o_ref[0, 0] = out.reshape(n_grp, n_row, HEAD_DIM).astype(BF16)


def _compress(cmp_in, pos, w1, w2):
    B = cmp_in.shape[0]
    S = cmp_in.shape[3]
    n_row = S // CMP_STRIDE
    width = CMP_STRIDE * HEAD_DIM
    return pl.pallas_call(
        _compress_kernel,
        grid=(2, B),
        in_specs=[
            pl.BlockSpec((1, 1, N_KV, S, HEAD_DIM), lambda a, b: (b, a, 0, 0, 0)),
            pl.BlockSpec((1, 2, width), lambda a, b: (a, 0, 0)),
            pl.BlockSpec((1,) + w1.shape[1:], lambda a, b: (a, 0, 0)),
            pl.BlockSpec((1,) + w2.shape[1:], lambda a, b: (a, 0, 0)),
        ],
        out_specs=pl.BlockSpec((1, 1, N_KV, n_row, HEAD_DIM), lambda a, b: (b, a, 0, 0, 0)),
        out_shape=jax.ShapeDtypeStruct((B, 2, N_KV, n_row, HEAD_DIM), BF16),
        compiler_params=pltpu.CompilerParams(
            dimension_semantics=("parallel", "parallel"),
            vmem_limit_bytes=VMEM_LIMIT_BYTES),
        name="nsa_compress",
    )(cmp_in, pos, w1, w2)


def _gate_expansion():
    r = jnp.arange(2 * GATE_LANES)[None, :, None] % GATE_LANES
    c = jnp.arange(N_BRANCH * HPG * HEAD_DIM)[None, None, :]
    g = jnp.arange(N_KV)[:, None, None]
    br, h = c // (HPG * HEAD_DIM), (c // HEAD_DIM) % HPG
    return (r == g * HPG * N_BRANCH + h * N_BRANCH + br).astype(BF16)


def _loop_chunks(tile, tq, kb):
    return (tile * tq) // kb


def _attn_pair_kernel(slope_ref, qa_ref, qb_ref, cmp_ref, kx_ref, vx_ref, ga_ref, gb_ref, za_ref, zb_ref, expand_ref,
                      oa_ref, ob_ref, qs_ref, m_ref, acc_ref):
    tq = qa_ref.shape[1]
    n_cmp_pad = cmp_ref.shape[3]
    seq = kx_ref.shape[3]
    n_sel = seq // SEL_BLOCK
    n_tiles = seq // tq
    kb = ATT_KB
    n_pair = HPG // 2
    grp = pl.program_id(1)
    pstep = pl.program_id(2)
    assert kb == 2 * tq and kb - tq == WINDOW and EXT_LANES - EXT_ONEHOT == n_sel and n_tiles % 2 == 0
    n_static = _loop_chunks(0, tq, kb) + _loop_chunks(n_tiles - 1, tq, kb)
    assert all(_loop_chunks(2 * i, tq, kb) + _loop_chunks(n_tiles - 1 - 2 * i, tq, kb) == n_static
               for i in range(n_tiles // 2))

    lane = lax.broadcasted_iota(jnp.int32, (1, LANES), 1)
    low = lane < HEAD_DIM
    first_head = jnp.bitwise_and(lax.broadcasted_iota(jnp.int32, (1, 2 * LANES), 1), HEAD_DIM) == 0

    k_cmp = cmp_ref[0, 0, 0].astype(F32)
    v_cmp = cmp_ref[0, 1, 0].astype(F32)
    zero_half = jnp.zeros_like(v_cmp)
    k_cmp_x = jnp.concatenate([k_cmp, zero_half], axis=1).astype(BF16)
    v_cmp_x = (jnp.concatenate([v_cmp, zero_half], axis=1).astype(BF16),
               jnp.concatenate([zero_half, v_cmp], axis=1).astype(BF16))
    j_ov = lax.broadcasted_iota(jnp.int32, (n_sel, n_cmp_pad), 0) * SEL_BLOCK
    c_ov = lax.broadcasted_iota(jnp.int32, (n_sel, n_cmp_pad), 1) * CMP_STRIDE
    overlap_t = jnp.where(c_ov < j_ov + SEL_BLOCK, jnp.where(c_ov + CMP_BLOCK > j_ov, 1.0, 0.0), 0.0).astype(BF16)
    slope_rows, slopes = [], []
    for h in range(HPG):
        base = (grp * HPG + h) * SLOPE_ROW
        row = jnp.zeros((1, LANES), F32)
        for j in range(2 * N_SLOPE_PIECES):
            row = jnp.where(lane == HEAD_DIM + j, slope_ref[base + j], row)
        slope_rows.append(row)
        slopes.append(slope_ref[base + 2 * N_SLOPE_PIECES])
    c_row = lax.broadcasted_iota(jnp.int32, (1, n_cmp_pad), 1)
    j_blk = lax.broadcasted_iota(jnp.int32, (n_sel, tq), 0)
    j_loc = lax.broadcasted_iota(jnp.int32, (SUBLANES, tq), 0)
    n_tile = n_sel // SUBLANES
    top_n = float(min(SEL_TOPN, n_sel))

    def tile_lanes(x, n):
        return jnp.concatenate([x] * n, axis=1)

    def prepare(q_ref, q0):
        t_col = q0 + lax.broadcasted_iota(jnp.int32, (tq, 1), 0)
        qf = q_ref[0].astype(F32)
        q_src = []
        for pair in range(n_pair):
            both = qf[:, pair * LANES:(pair + 1) * LANES]
            q_src += [both, pltpu.roll(both, HEAD_DIM, axis=1)]
        q_win = [jnp.where(low, q_src[h], slope_rows[h]).astype(BF16) for h in range(HPG)]

        dist_c = t_col - (c_row * CMP_STRIDE + (CMP_BLOCK - 1))
        valid_c = dist_c >= 0
        dist_cf = dist_c.astype(F32)
        p_sum = jnp.zeros((tq, n_cmp_pad), F32)
        o_cmp = []
        for pair in range(n_pair):
            o_pair = None
            for hh in range(2):
                h = 2 * pair + hh
                s = _dot_nt(q_src[h].astype(BF16), k_cmp_x) - slopes[h] * dist_cf
                s = jnp.where(valid_c, s, NEG_INF)
                m = jnp.max(s, axis=-1, keepdims=True)
                p = jnp.where(valid_c, jnp.exp2(s - m), 0.0)
                l = jnp.sum(p, axis=-1, keepdims=True)
                p = p * jnp.where(l > 0.0, 1.0 / l, 0.0)
                p_sum = p_sum + p
                d = _dot(p.astype(BF16), v_cmp_x[hh])
                o_pair = d if o_pair is None else o_pair + d
            o_cmp.append(o_pair)

        p_hi = p_sum.astype(BF16)
        p_lo = (p_sum - p_hi.astype(F32)).astype(BF16)
        imp = _dot_nt(overlap_t, p_hi) + _dot_nt(overlap_t, p_lo)
        cur = _div_pow2(q0 + lax.broadcasted_iota(jnp.int32, (n_sel, tq), 1), SEL_BLOCK)
        forced = (j_blk == 0) | (j_blk == cur) | (j_blk == cur - 1)
        imp = jnp.where(forced, FORCE_SCORE, jnp.where(j_blk > cur, NEG_INF, imp))
        tiles = [imp[v * SUBLANES:(v + 1) * SUBLANES, :] for v in range(n_tile)]
        rank = [jnp.zeros((SUBLANES, tq), F32) for _ in range(n_tile)]
        for i in range(n_sel):
            v, i_loc = divmod(i, SUBLANES)
            row = tiles[v][i_loc:i_loc + 1, :]
            for w in range(n_tile):
                if w > v:
                    ahead = jnp.where(row >= tiles[w], 1.0, 0.0)
                elif w < v:
                    ahead = jnp.where(row > tiles[w], 1.0, 0.0)
                else:
                    ahead = jnp.where(j_loc > i_loc, jnp.where(row >= tiles[w], 1.0, 0.0),
                                      jnp.where(row > tiles[w], 1.0, 0.0))
                rank[w] = rank[w] + ahead
        mask_rows = [jnp.zeros((HEAD_DIM + EXT_ONEHOT, tq), F32)] + [jnp.where(r < top_n, 0.0, NEG_INF) for r in rank]
        mask_ext = jnp.transpose(jnp.concatenate(mask_rows, axis=0))
        q_aug = [jnp.where(low, q_src[h], mask_ext + slope_rows[h]).astype(BF16) for h in range(HPG)]
        return t_col, o_cmp, q_aug, q_win

    def chunk_init(queries, which_k, v_base, k0, n_keys, bias):
        k_c = kx_ref[0, which_k, 0, pl.ds(k0, n_keys), :]
        ms, accs = [], []
        for pair in range(n_pair):
            pv = None
            for hh in range(2):
                s = _dot_nt(queries[2 * pair + hh], k_c) + bias
                m = jnp.max(s, axis=-1, keepdims=True)
                p = jnp.exp2(s - m).astype(BF16)
                d = _dot(p, vx_ref[0, v_base + hh, 0, pl.ds(k0, n_keys), :])
                pv = d if pv is None else pv + d
                ms.append(m)
            accs.append(pv)
        return ms, accs

    tile_of = (2 * pstep, n_tiles - 1 - 2 * pstep)
    q_refs = (qa_ref, qb_ref)
    n_loop, o_cmp, acc_win = [], [], []
    for t in range(2):
        q0 = tile_of[t] * tq
        t_col, o_cmp_t, q_aug, q_win = prepare(q_refs[t], q0)
        o_cmp.append(o_cmp_t)
        n_loop.append(_div_pow2(q0, kb))
        k0 = pl.multiple_of(jnp.maximum(q0 - WINDOW, 0), LANES)
        pos = k0 + lax.broadcasted_iota(jnp.int32, (1, kb), 1)
        win_bias = jnp.where(pos <= t_col, jnp.where(pos > t_col - WINDOW, 0.0, NEG_INF), NEG_INF)
        acc_win.append(chunk_init(q_win, 1, 2, k0, kb, win_bias)[1])
        if t == 0:
            s0, n_keys = pl.multiple_of(q0, LANES), tq
        else:
            s0, n_keys = k0, kb
        pos_s = s0 + lax.broadcasted_iota(jnp.int32, (1, n_keys), 1)
        sel_bias = jnp.where(pos_s <= t_col, 0.0, NEG_INF)
        m_sel, acc_sel = chunk_init(q_aug, 0, 0, s0, n_keys, sel_bias)
        for h in range(HPG):
            m_ref[t, h] = jnp.broadcast_to(m_sel[h], (tq, LANES))
            qs_ref[t, h] = q_aug[h]
        for pair in range(n_pair):
            acc_ref[t, pair] = acc_sel[pair]

    for j in range(n_static):
        mine = j < n_loop[0]
        t = jnp.where(mine, 0, 1)
        kc0 = pl.multiple_of(jnp.where(mine, j, j - n_loop[0]) * kb, kb)
        k_c = kx_ref[0, 0, 0, pl.ds(kc0, kb), :]
        for pair in range(n_pair):
            pv, scale = None, []
            for hh in range(2):
                h = 2 * pair + hh
                s = _dot_nt(qs_ref[t, h], k_c)
                m_old = m_ref[t, h]
                m_new = jnp.maximum(m_old, jnp.max(s, axis=-1, keepdims=True))
                m_ref[t, h] = m_new
                scale.append(tile_lanes(jnp.exp2(m_old - m_new), 2))
                p = jnp.exp2(s - tile_lanes(m_new, kb // LANES)).astype(BF16)
                d = _dot(p, vx_ref[0, hh, 0, pl.ds(kc0, kb), :])
                pv = d if pv is None else pv + d
            acc_ref[t, pair] = jnp.where(first_head, scale[0], scale[1]) * acc_ref[t, pair] + pv

    width = HPG * HEAD_DIM
    for t, (g_ref, z_ref, o_ref) in enumerate(((ga_ref, za_ref, oa_ref), (gb_ref, zb_ref, ob_ref))):
        gates = g_ref[0]
        g_hi = gates.astype(BF16)
        g_lo = (gates - g_hi.astype(F32)).astype(BF16)
        g_exp = _dot(jnp.concatenate([g_hi, g_lo], axis=1), expand_ref[0])
        outs = []
        for pair in range(n_pair):
            a_sel = acc_ref[t, pair]
            a_win = acc_win[t][pair]
            lo = pair * LANES
            outs.append(g_exp[:, lo:lo + LANES] * o_cmp[t][pair]
                        + g_exp[:, width + lo:width + lo + LANES] * (a_sel[:, :LANES] / a_sel[:, LANES:])
                        + g_exp[:, 2 * width + lo:2 * width + lo + LANES] * (a_win[:, :LANES] / a_win[:, LANES:]))
        o_ref[0] = (jnp.concatenate(outs, axis=1) * _silu(z_ref[0])).astype(BF16)


def _attention_pairs(slope_table, q, cmp_kv, kx, vx, gates, z):
    B, S, _ = q.shape
    tq = ATT_TQ
    n_tiles = S // tq
    n_half = n_tiles // 2
    n_cmp_pad = cmp_kv.shape[3]
    expand = _gate_expansion()
    width = HPG * HEAD_DIM

    def even(b, g, i, sl):
        return (b, 2 * i, g)

    def odd(b, g, i, sl):
        return (b, n_tiles - 1 - 2 * i, g)

    grid_spec = pltpu.PrefetchScalarGridSpec(
        num_scalar_prefetch=1,
        grid=(B, N_KV, n_half),
        in_specs=[
            pl.BlockSpec((1, tq, width), even),
            pl.BlockSpec((1, tq, width), odd),
            pl.BlockSpec((1, 2, 1, n_cmp_pad, HEAD_DIM), lambda b, g, i, sl: (b, 0, g, 0, 0)),
            pl.BlockSpec((1, 2, 1, S, LANES), lambda b, g, i, sl: (b, 0, g, 0, 0)),
            pl.BlockSpec((1, 4, 1, S, 2 * LANES), lambda b, g, i, sl: (b, 0, g, 0, 0)),
            pl.BlockSpec((1, tq, GATE_LANES), lambda b, g, i, sl: (b, 2 * i, 0)),
            pl.BlockSpec((1, tq, GATE_LANES), lambda b, g, i, sl: (b, n_tiles - 1 - 2 * i, 0)),
            pl.BlockSpec((1, tq, width), even),
            pl.BlockSpec((1, tq, width), odd),
            pl.BlockSpec((1,) + expand.shape[1:], lambda b, g, i, sl: (g, 0, 0)),
        ],
        out_specs=[
            pl.BlockSpec((1, tq, width), lambda b, g, i, sl: (b, i, g)),
            pl.BlockSpec((1, tq, width), lambda b, g, i, sl: (b, n_half - 1 - i, g)),
        ],
        scratch_shapes=[
            pltpu.VMEM((2, HPG, tq, LANES), BF16),
            pltpu.VMEM((2, HPG, tq, LANES), F32),
            pltpu.VMEM((2, HPG // 2, tq, 2 * LANES), F32),
        ],
    )
    half = jax.ShapeDtypeStruct((B, S // 2, D_ATT), BF16)
    return pl.pallas_call(
        _attn_pair_kernel,
        grid_spec=grid_spec,
        out_shape=[half, half],
        compiler_params=pltpu.CompilerParams(
            dimension_semantics=("parallel", "parallel", "arbitrary"),
            vmem_limit_bytes=VMEM_LIMIT_BYTES),
        name="nsa_attention",
    )(slope_table, q, q, cmp_kv, kx, vx, gates, gates, z, z, expand)


def _nsa_out_kernel(o_even_ref, o_odd_ref, x_ref, w_ref, g_ref, b_ref, out_ref):
    n_tile, tq, d = o_even_ref.shape
    gated = jnp.concatenate([ref[i] for i in range(n_tile) for ref in (o_even_ref, o_odd_ref)], axis=0)
    y = _dot(gated, w_ref[...])
    out_ref[0] = _layer_norm(DN_ALPHA * x_ref[0] + y, g_ref[...], b_ref[...])


def _nsa_out(o_even, o_odd, x, w_out, ln_g, ln_b):
    B, S, D = x.shape
    tm, tq = PROJ_TM, ATT_TQ
    n_tile = tm // (2 * tq)
    assert tm == n_tile * 2 * tq
    tile = pl.BlockSpec((1, tm, D), lambda b, s: (b, s, 0))
    o_tile = pl.BlockSpec((n_tile, tq, D), lambda b, s: (b * (S // tm) + s, 0, 0))
    o_even = o_even.reshape(B * S // (2 * tq), tq, D)
    o_odd = o_odd.reshape(B * S // (2 * tq), tq, D)
    return pl.pallas_call(
        _nsa_out_kernel,
        grid=(B, S // tm),
        in_specs=[o_tile, o_tile, tile,
                  _const_spec(w_out.shape), _const_spec(ln_g.shape), _const_spec(ln_b.shape)],
        out_specs=tile,
        out_shape=jax.ShapeDtypeStruct((B, S, D), F32),
        compiler_params=pltpu.CompilerParams(
            dimension_semantics=("parallel", "parallel"),
            vmem_limit_bytes=VMEM_LIMIT_BYTES),
        name="nsa_out",
    )(o_even, o_odd, x, w_out, ln_g, ln_b)


def _slope_table():
    h = jnp.arange(1, N_HEADS + 1, dtype=F32)
    slope = (2.0 ** (-8.0 * h / N_HEADS)) * LOG2E
    pieces, rest = [], slope
    for _ in range(N_SLOPE_PIECES):
        piece = rest.astype(BF16).astype(F32)
        pieces.append(piece)
        rest = rest - piece
    cols = [SEL_BLOCK * p for p in pieces] + pieces + [slope]
    cols += [jnp.zeros_like(slope)] * (SLOPE_ROW - len(cols))
    return jnp.stack(cols, axis=1).reshape(-1)


def kernel(x, ln_g, ln_b, pool_w_in, pool_w_grp, pool_scale, pool_w_out, nsa_w_in, nsa_cmp_pos_k, nsa_cmp_w1_k,
           nsa_cmp_w2_k, nsa_cmp_pos_v, nsa_cmp_w1_v, nsa_cmp_w2_v, nsa_w_out):
    seq = x.shape[1]
    assert seq % max(POOL_TM, PROJ_TM, 2 * ATT_TQ) == 0 and x.shape[2] == D_MODEL
    assert seq >= WINDOW + ATT_TQ and WINDOW % ATT_TQ == 0 and ATT_TQ % SEL_BLOCK == 0
    assert seq // SEL_BLOCK == EXT_LANES - EXT_ONEHOT

    x = _pool_layer(x, pool_w_in[0].astype(BF16), pool_w_grp[0].astype(BF16), pool_scale[0][None, :],
                    pool_w_out[0].astype(BF16), ln_g[0][None, :], ln_b[0][None, :])

    w = nsa_w_in[0]
    kv_lo, z_lo, gl_lo = D_ATT, D_ATT + 6 * D_KV, 2 * D_ATT + 6 * D_KV
    wq = w[:, :kv_lo].astype(BF16)
    wkv = w[:, kv_lo:z_lo].astype(BF16)
    wz = w[:, z_lo:gl_lo].astype(BF16)
    wgl = jnp.pad(w[:, gl_lo:], ((0, 0), (0, GATE_LANES - N_BRANCH * N_HEADS))).astype(BF16)
    q, cmp_in, kx, vx, z, gates = _nsa_inproj(x, wq, wkv, wz, wgl)

    half = CMP_BLOCK // 2
    pos = jnp.stack([nsa_cmp_pos_k[0], nsa_cmp_pos_v[0]]).reshape(2, 2, half * HEAD_DIM)
    w1 = jnp.stack([nsa_cmp_w1_k[0], nsa_cmp_w1_v[0]]).astype(BF16)
    w2 = jnp.stack([nsa_cmp_w2_k[0], nsa_cmp_w2_v[0]]).astype(BF16)
    cmp_kv = _compress(cmp_in, pos, w1, w2)

    o_even, o_odd = _attention_pairs(_slope_table(), q, cmp_kv, kx, vx, gates, z)
    return _nsa_out(o_even, o_odd, x, nsa_w_out[0].astype(BF16), ln_g[1][None, :], ln_b[1][None, :])
```

```python
import math

import jax
import jax.numpy as jnp
from jax import lax
from jax.experimental import pallas as pl
from jax.experimental.pallas import tpu as pltpu

F32 = jnp.float32
BF16 = jnp.bfloat16

D_MODEL = 1024
DEPTH = 2
D_POOL = 2 * D_MODEL
POOL_WINDOWS = (2, 4, 8, 16)
POOL_GROUP = D_POOL // len(POOL_WINDOWS)
HEAD_DIM = 64
N_HEADS = D_MODEL // HEAD_DIM
N_KV = 4
HPG = N_HEADS // N_KV
D_ATT = N_HEADS * HEAD_DIM
D_KV = N_KV * HEAD_DIM
N_BRANCH = 3
CMP_BLOCK = 32
CMP_STRIDE = 16
SEL_BLOCK = 64
SEL_TOPN = 8
WINDOW = 256
DN_ALPHA = (2.0 * DEPTH) ** 0.25
LN_EPS = 1e-5
NEG_INF = -1e30
FORCE_SCORE = 1e9
LOG2E = math.log2(math.e)

LANES = 128
SUBLANES = 8
VMEM_LIMIT_BYTES = 56 * 1024 * 1024

POOL_TM = 512
POOL_HALO = 32
PROJ_TM = 512
ATT_TQ = 256
ATT_KB = 512
GATE_LANES = LANES

N_SLOPE_PIECES = 3
EXT_ONEHOT = 32
SLOPE_ROW = 8
EXT_LANES = LANES - HEAD_DIM


def _dot(a, b):
    return jnp.dot(a, b, preferred_element_type=F32)


def _dot_nt(a, b):
    return lax.dot_general(a, b, (((1,), (1,)), ((), ())), preferred_element_type=F32)


def _div_pow2(x, n):
    assert n & (n - 1) == 0
    return lax.shift_right_logical(x, n.bit_length() - 1)


def _silu(x):
    return x * jax.nn.sigmoid(x)


def _layer_norm(r, g, b):
    mu = jnp.mean(r, axis=-1, keepdims=True)
    d = r - mu
    var = jnp.mean(d * d, axis=-1, keepdims=True)
    return d * lax.rsqrt(var + LN_EPS) * g + b


def _const_spec(shape):
    n = len(shape)
    return pl.BlockSpec(shape, lambda *_: (0,) * n, pipeline_mode=pl.Buffered(1))


def _pool_layer_kernel(x_ref, w_in_ref, w_grp_ref, scale_ref, w_out_ref, g_ref, b_ref,
                       o_ref, carry_ref, work_a_ref, work_b_ref):
    tm = x_ref.shape[1]
    work = (work_a_ref, work_b_ref)
    s = pl.program_id(1)

    @pl.when(s == 0)
    def _():
        carry_ref[...] = jnp.zeros_like(carry_ref)

    xt = x_ref[0]
    xb = xt.astype(BF16)
    t = s * tm + lax.broadcasted_iota(jnp.int32, (tm, 1), 0)
    y = jnp.zeros((tm, D_MODEL), F32)
    for g, w in enumerate(POOL_WINDOWS):
        lo = g * POOL_GROUP
        u = _dot(xb, w_in_ref[:, lo:lo + POOL_GROUP])
        z = _dot(xb, w_in_ref[:, D_POOL + lo:D_POOL + lo + POOL_GROUP])
        work[0][0:POOL_HALO, :] = carry_ref[g]
        work[0][POOL_HALO:, :] = u
        carry_ref[g] = u[tm - POOL_HALO:, :]
        n_steps = w.bit_length() - 1
        assert w == 1 << n_steps and SUBLANES * n_steps <= POOL_HALO
        src = 0
        for j in range(n_steps - 1):
            shift, lo_row = 1 << j, SUBLANES * (j + 1)
            n = POOL_HALO + tm - lo_row
            work[1 - src][lo_row:, :] = work[src][lo_row:, :] + work[src][lo_row - shift:lo_row - shift + n, :]
            src = 1 - src
        shift = w // 2
        tot = work[src][POOL_HALO:, :] + work[src][POOL_HALO - shift:POOL_HALO - shift + tm, :]
        inv_cnt = 1.0 / jnp.minimum(t + 1, w).astype(F32)
        pooled = tot * inv_cnt - u
        m = _dot(pooled.astype(BF16), w_grp_ref[g]) * scale_ref[:, lo:lo + POOL_GROUP]
        gated = (m * _silu(z)).astype(BF16)
        if g + 1 < len(POOL_WINDOWS):
            y = y + _dot(gated, w_out_ref[lo:lo + POOL_GROUP, :])
        else:
            for r0 in range(0, tm, tm // 2):
                rows = slice(r0, r0 + tm // 2)
                y_rows = y[rows] + _dot(gated[rows], w_out_ref[lo:lo + POOL_GROUP, :])
                o_ref[0, rows, :] = _layer_norm(DN_ALPHA * xt[rows] + y_rows, g_ref[...], b_ref[...])


def _pool_layer(x, w_in, w_grp, scale, w_out, ln_g, ln_b):
    B, S, D = x.shape
    tm = POOL_TM
    return pl.pallas_call(
        _pool_layer_kernel,
        grid=(B, S // tm),
        in_specs=[
            pl.BlockSpec((1, tm, D), lambda b, s: (b, s, 0)),
            _const_spec(w_in.shape),
            _const_spec(w_grp.shape),
            _const_spec(scale.shape),
            _const_spec(w_out.shape),
            _const_spec(ln_g.shape),
            _const_spec(ln_b.shape),
        ],
        out_specs=pl.BlockSpec((1, tm, D), lambda b, s: (b, s, 0)),
        out_shape=jax.ShapeDtypeStruct((B, S, D), F32),
        scratch_shapes=[
            pltpu.VMEM((len(POOL_WINDOWS), POOL_HALO, POOL_GROUP), F32),
            pltpu.VMEM((POOL_HALO + tm, POOL_GROUP), F32),
            pltpu.VMEM((POOL_HALO + tm, POOL_GROUP), F32),
        ],
        compiler_params=pltpu.CompilerParams(
            dimension_semantics=("parallel", "arbitrary"),
            vmem_limit_bytes=VMEM_LIMIT_BYTES),
        name="pool_layer",
    )(x, w_in, w_grp, scale, w_out, ln_g, ln_b)


def _nsa_inproj_kernel(x_ref, wq_ref, wkv_ref, wz_ref, wgl_ref,
                       q_ref, cmp_ref, kx_ref, vx_ref, z_ref, gate_ref):
    tm = x_ref.shape[1]
    xb = x_ref[0].astype(BF16)
    q_ref[0] = (_dot(xb, wq_ref[...]) * (HEAD_DIM ** -0.5 * LOG2E)).astype(BF16)
    kv = _dot(xb, wkv_ref[...])
    for a in range(2):
        for g in range(N_KV):
            lo = a * D_KV + g * HEAD_DIM
            cmp_ref[0, a, g] = kv[:, lo:lo + HEAD_DIM]

    lane = lax.broadcasted_iota(jnp.int32, (tm, LANES), 1)
    ext = lane - HEAD_DIM
    low = ext < 0
    pos = pl.program_id(1) * tm + lax.broadcasted_iota(jnp.int32, (tm, LANES), 0)
    blk = _div_pow2(pos, SEL_BLOCK)
    alibi = jnp.where(ext < N_SLOPE_PIECES, blk, jnp.where(ext < 2 * N_SLOPE_PIECES, pos - blk * SEL_BLOCK, 0))
    k_win_ext = alibi.astype(F32)
    k_sel_ext = jnp.where(ext >= EXT_ONEHOT, jnp.where(ext - EXT_ONEHOT == blk, 1.0, 0.0), k_win_ext)
    ones_a = jnp.where(low, 1.0, 0.0).astype(BF16)
    ones_b = jnp.where(low, 0.0, 1.0).astype(BF16)
    for a in range(2, 6):
        for pair in range(N_KV // 2):
            lo = a * D_KV + pair * LANES
            both = kv[:, lo:lo + LANES]
            swapped = pltpu.roll(both, HEAD_DIM, axis=1)
            for g, (at_low, at_high) in ((2 * pair, (both, swapped)), (2 * pair + 1, (swapped, both))):
                if a % 2 == 0:
                    k_ext = k_sel_ext if a == 2 else k_win_ext
                    kx_ref[0, (a - 2) // 2, g] = jnp.where(low, at_low, k_ext).astype(BF16)
                else:
                    va = jnp.where(low, at_low, 0.0).astype(BF16)
                    vb = jnp.where(low, 0.0, at_high).astype(BF16)
                    vx_ref[0, a - 3, g] = jnp.concatenate([va, ones_a], axis=1)
                    vx_ref[0, a - 2, g] = jnp.concatenate([vb, ones_b], axis=1)

    z_ref[0] = _dot(xb, wz_ref[...])
    gate_ref[0] = jax.nn.sigmoid(_dot(xb, wgl_ref[...]))


def _nsa_inproj(x, wq, wkv, wz, wgl):
    B, S, D = x.shape
    tm = PROJ_TM
    return pl.pallas_call(
        _nsa_inproj_kernel,
        grid=(B, S // tm),
        in_specs=[
            pl.BlockSpec((1, tm, D), lambda b, s: (b, s, 0)),
            _const_spec(wq.shape),
            _const_spec(wkv.shape),
            _const_spec(wz.shape),
            _const_spec(wgl.shape),
        ],
        out_specs=[
            pl.BlockSpec((1, tm, D_ATT), lambda b, s: (b, s, 0)),
            pl.BlockSpec((1, 2, N_KV, tm, HEAD_DIM), lambda b, s: (b, 0, 0, s, 0)),
            pl.BlockSpec((1, 2, N_KV, tm, LANES), lambda b, s: (b, 0, 0, s, 0)),
            pl.BlockSpec((1, 4, N_KV, tm, 2 * LANES), lambda b, s: (b, 0, 0, s, 0)),
            pl.BlockSpec((1, tm, D_ATT), lambda b, s: (b, s, 0)),
            pl.BlockSpec((1, tm, GATE_LANES), lambda b, s: (b, s, 0)),
        ],
        out_shape=[
            jax.ShapeDtypeStruct((B, S, D_ATT), BF16),
            jax.ShapeDtypeStruct((B, 2, N_KV, S, HEAD_DIM), F32),
            jax.ShapeDtypeStruct((B, 2, N_KV, S, LANES), BF16),
            jax.ShapeDtypeStruct((B, 4, N_KV, S, 2 * LANES), BF16),
            jax.ShapeDtypeStruct((B, S, D_ATT), F32),
            jax.ShapeDtypeStruct((B, S, GATE_LANES), F32),
        ],
        compiler_params=pltpu.CompilerParams(
            dimension_semantics=("parallel", "parallel"),
            vmem_limit_bytes=VMEM_LIMIT_BYTES),
        name="nsa_inproj",
    )(x, wq, wkv, wz, wgl)


def _compress_kernel(c_ref, pos_ref, w1_ref, w2_ref, o_ref):
    n_grp, seq = c_ref.shape[2:4]
    n_row = seq // CMP_STRIDE
    width = CMP_STRIDE * HEAD_DIM
    groups = []
    for g in range(n_grp):
        pieces = [c_ref[0, 0, g, pl.ds(l, n_row, stride=CMP_STRIDE), :] for l in range(CMP_STRIDE)]
        groups.append(jnp.concatenate(pieces, axis=1))
    rows = jnp.concatenate(groups, axis=0)
    nxt = pltpu.roll(rows, shift=n_grp * n_row - 1, axis=0)
    first = (rows + pos_ref[0, 0:1, :]).astype(BF16)
    second = (nxt + pos_ref[0, 1:2, :]).astype(BF16)
    h = _dot(first, w1_ref[0, :width, :]) + _dot(second, w1_ref[0, width:, :])
    out = _dot(_silu(h).astype(BF16), w2_ref[0])
    o_ref[0, 0] = out.reshape(n_grp, n_row, HEAD_DIM).astype(BF16)


def _compress(cmp_in, pos, w1, w2):
    B = cmp_in.shape[0]
    S = cmp_in.shape[3]
    n_row = S // CMP_STRIDE
    width = CMP_STRIDE * HEAD_DIM
    return pl.pallas_call(
        _compress_kernel,
        grid=(2, B),
        in_specs=[
            pl.BlockSpec((1, 1, N_KV, S, HEAD_DIM), lambda a, b: (b, a, 0, 0, 0)),
            pl.BlockSpec((1, 2, width), lambda a, b: (a, 0, 0)),
            pl.BlockSpec((1,) + w1.shape[1:], lambda a, b: (a, 0, 0)),
            pl.BlockSpec((1,) + w2.shape[1:], lambda a, b: (a, 0, 0)),
        ],
        out_specs=pl.BlockSpec((1, 1, N_KV, n_row, HEAD_DIM), lambda a, b: (b, a, 0, 0, 0)),
        out_shape=jax.ShapeDtypeStruct((B, 2, N_KV, n_row, HEAD_DIM), BF16),
        compiler_params=pltpu.CompilerParams(
            dimension_semantics=("parallel", "parallel"),
            vmem_limit_bytes=VMEM_LIMIT_BYTES),
        name="nsa_compress",
    )(cmp_in, pos, w1, w2)


def _gate_expansion():
    r = jnp.arange(2 * GATE_LANES)[None, :, None] % GATE_LANES
    c = jnp.arange(N_BRANCH * HPG * HEAD_DIM)[None, None, :]
    g = jnp.arange(N_KV)[:, None, None]
    br, h = c // (HPG * HEAD_DIM), (c // HEAD_DIM) % HPG
    return (r == g * HPG * N_BRANCH + h * N_BRANCH + br).astype(BF16)


def _loop_chunks(tile, tq, kb):
    return (tile * tq) // kb


def _attn_pair_kernel(slope_ref, qa_ref, qb_ref, cmp_ref, kx_ref, vx_ref, ga_ref, gb_ref, za_ref, zb_ref, expand_ref,
                      oa_ref, ob_ref, qs_ref, m_ref, acc_ref):
    tq = qa_ref.shape[1]
    n_cmp_pad = cmp_ref.shape[3]
    seq = kx_ref.shape[3]
    n_sel = seq // SEL_BLOCK
    n_tiles = seq // tq
    kb = ATT_KB
    n_pair = HPG // 2
    grp = pl.program_id(1)
    pstep = pl.program_id(2)
    assert kb == 2 * tq and kb - tq == WINDOW and EXT_LANES - EXT_ONEHOT == n_sel and n_tiles % 2 == 0
    n_static = _loop_chunks(0, tq, kb) + _loop_chunks(n_tiles - 1, tq, kb)
    assert all(_loop_chunks(2 * i, tq, kb) + _loop_chunks(n_tiles - 1 - 2 * i, tq, kb) == n_static
               for i in range(n_tiles // 2))

    lane = lax.broadcasted_iota(jnp.int32, (1, LANES), 1)
    low = lane < HEAD_DIM
    first_head = jnp.bitwise_and(lax.broadcasted_iota(jnp.int32, (1, 2 * LANES), 1), HEAD_DIM) == 0

    k_cmp = cmp_ref[0, 0, 0].astype(F32)
    v_cmp = cmp_ref[0, 1, 0].astype(F32)
    zero_half = jnp.zeros_like(v_cmp)
    k_cmp_x = jnp.concatenate([k_cmp, zero_half], axis=1).astype(BF16)
    v_cmp_x = (jnp.concatenate([v_cmp, zero_half], axis=1).astype(BF16),
               jnp.concatenate([zero_half, v_cmp], axis=1).astype(BF16))
    j_ov = lax.broadcasted_iota(jnp.int32, (n_sel, n_cmp_pad), 0) * SEL_BLOCK
    c_ov = lax.broadcasted_iota(jnp.int32, (n_sel, n_cmp_pad), 1) * CMP_STRIDE
    overlap_t = jnp.where(c_ov < j_ov + SEL_BLOCK, jnp.where(c_ov + CMP_BLOCK > j_ov, 1.0, 0.0), 0.0).astype(BF16)
    slope_rows, slopes = [], []
    for h in range(HPG):
        base = (grp * HPG + h) * SLOPE_ROW
        row = jnp.zeros((1, LANES), F32)
        for j in range(2 * N_SLOPE_PIECES):
            row = jnp.where(lane == HEAD_DIM + j, slope_ref[base + j], row)
        slope_rows.append(row)
        slopes.append(slope_ref[base + 2 * N_SLOPE_PIECES])
    c_row = lax.broadcasted_iota(jnp.int32, (1, n_cmp_pad), 1)
    j_blk = lax.broadcasted_iota(jnp.int32, (n_sel, tq), 0)
    j_loc = lax.broadcasted_iota(jnp.int32, (SUBLANES, tq), 0)
    n_tile = n_sel // SUBLANES
    top_n = float(min(SEL_TOPN, n_sel))

    def tile_lanes(x, n):
        return jnp.concatenate([x] * n, axis=1)

    def prepare(q_ref, q0):
        t_col = q0 + lax.broadcasted_iota(jnp.int32, (tq, 1), 0)
        qf = q_ref[0].astype(F32)
        q_src = []
        for pair in range(n_pair):
            both = qf[:, pair * LANES:(pair + 1) * LANES]
            q_src += [both, pltpu.roll(both, HEAD_DIM, axis=1)]
        q_win = [jnp.where(low, q_src[h], slope_rows[h]).astype(BF16) for h in range(HPG)]

        dist_c = t_col - (c_row * CMP_STRIDE + (CMP_BLOCK - 1))
        valid_c = dist_c >= 0
        dist_cf = dist_c.astype(F32)
        p_sum = jnp.zeros((tq, n_cmp_pad), F32)
        o_cmp = []
        for pair in range(n_pair):
            o_pair = None
            for hh in range(2):
                h = 2 * pair + hh
                s = _dot_nt(q_src[h].astype(BF16), k_cmp_x) - slopes[h] * dist_cf
                s = jnp.where(valid_c, s, NEG_INF)
                m = jnp.max(s, axis=-1, keepdims=True)
                p = jnp.where(valid_c, jnp.exp2(s - m), 0.0)
                l = jnp.sum(p, axis=-1, keepdims=True)
                p = p * jnp.where(l > 0.0, 1.0 / l, 0.0)
                p_sum = p_sum + p
                d = _dot(p.astype(BF16), v_cmp_x[hh])
                o_pair = d if o_pair is None else o_pair + d
            o_cmp.append(o_pair)

        p_hi = p_sum.astype(BF16)
        p_lo = (p_sum - p_hi.astype(F32)).astype(BF16)
        imp = _dot_nt(overlap_t, p_hi) + _dot_nt(overlap_t, p_lo)
        cur = _div_pow2(q0 + lax.broadcasted_iota(jnp.int32, (n_sel, tq), 1), SEL_BLOCK)
        forced = (j_blk == 0) | (j_blk == cur) | (j_blk == cur - 1)
        imp = jnp.where(forced, FORCE_SCORE, jnp.where(j_blk > cur, NEG_INF, imp))
        tiles = [imp[v * SUBLANES:(v + 1) * SUBLANES, :] for v in range(n_tile)]
        rank = [jnp.zeros((SUBLANES, tq), F32) for _ in range(n_tile)]
        for i in range(n_sel):
            v, i_loc = divmod(i, SUBLANES)
            row = tiles[v][i_loc:i_loc + 1, :]
            for w in range(n_tile):
                if w > v:
                    ahead = jnp.where(row >= tiles[w], 1.0, 0.0)
                elif w < v:
                    ahead = jnp.where(row > tiles[w], 1.0, 0.0)
                else:
                    ahead = jnp.where(j_loc > i_loc, jnp.where(row >= tiles[w], 1.0, 0.0),
                                      jnp.where(row > tiles[w], 1.0, 0.0))
                rank[w] = rank[w] + ahead
        mask_rows = [jnp.zeros((HEAD_DIM + EXT_ONEHOT, tq), F32)] + [jnp.where(r < top_n, 0.0, NEG_INF) for r in rank]
        mask_ext = jnp.transpose(jnp.concatenate(mask_rows, axis=0))
        q_aug = [jnp.where(low, q_src[h], mask_ext + slope_rows[h]).astype(BF16) for h in range(HPG)]
        return t_col, o_cmp, q_aug, q_win

    def chunk_init(queries, which_k, v_base, k0, n_keys, bias):
        k_c = kx_ref[0, which_k, 0, pl.ds(k0, n_keys), :]
        ms, accs = [], []
        for pair in range(n_pair):
            pv = None
            for hh in range(2):
                s = _dot_nt(queries[2 * pair + hh], k_c) + bias
                m = jnp.max(s, axis=-1, keepdims=True)
                p = jnp.exp2(s - m).astype(BF16)
                d = _dot(p, vx_ref[0, v_base + hh, 0, pl.ds(k0, n_keys), :])
                pv = d if pv is None else pv + d
                ms.append(m)
            accs.append(pv)
        return ms, accs

    tile_of = (2 * pstep, n_tiles - 1 - 2 * pstep)
    q_refs = (qa_ref, qb_ref)
    q0s = [tile_of[t] * tq for t in range(2)]
    n_loop = [_div_pow2(q0, kb) for q0 in q0s]
    k0s = [pl.multiple_of(jnp.maximum(q0 - WINDOW, 0), LANES) for q0 in q0s]
    prepared = [prepare(q_refs[t], q0s[t]) for t in range(2)]
    o_cmp = [p[1] for p in prepared]
    acc_win = []
    for t in range(2):
        t_col, _, _, q_win = prepared[t]
        pos = k0s[t] + lax.broadcasted_iota(jnp.int32, (1, kb), 1)
        win_bias = jnp.where(pos <= t_col, jnp.where(pos > t_col - WINDOW, 0.0, NEG_INF), NEG_INF)
        acc_win.append(chunk_init(q_win, 1, 2, k0s[t], kb, win_bias)[1])
    for t in range(2):
        t_col, _, q_aug, _ = prepared[t]
        if t == 0:
            s0, n_keys = pl.multiple_of(q0s[t], LANES), tq
        else:
            s0, n_keys = k0s[t], kb
        pos_s = s0 + lax.broadcasted_iota(jnp.int32, (1, n_keys), 1)
        sel_bias = jnp.where(pos_s <= t_col, 0.0, NEG_INF)
        m_sel, acc_sel = chunk_init(q_aug, 0, 0, s0, n_keys, sel_bias)
        for h in range(HPG):
            m_ref[t, h] = jnp.broadcast_to(m_sel[h], (tq, LANES))
            qs_ref[t, h] = q_aug[h]
        for pair in range(n_pair):
            acc_ref[t, pair] = acc_sel[pair]

    for j in range(n_static):
        mine = j < n_loop[0]
        t = jnp.where(mine, 0, 1)
        kc0 = pl.multiple_of(jnp.where(mine, j, j - n_loop[0]) * kb, kb)
        k_c = kx_ref[0, 0, 0, pl.ds(kc0, kb), :]
        for pair in range(n_pair):
            pv, scale = None, []
            for hh in range(2):
                h = 2 * pair + hh
                s = _dot_nt(qs_ref[t, h], k_c)
                m_old = m_ref[t, h]
                m_new = jnp.maximum(m_old, jnp.max(s, axis=-1, keepdims=True))
                m_ref[t, h] = m_new
                scale.append(tile_lanes(jnp.exp2(m_old - m_new), 2))
                p = jnp.exp2(s - tile_lanes(m_new, kb // LANES)).astype(BF16)
                d = _dot(p, vx_ref[0, hh, 0, pl.ds(kc0, kb), :])
                pv = d if pv is None else pv + d
            acc_ref[t, pair] = jnp.where(first_head, scale[0], scale[1]) * acc_ref[t, pair] + pv

    width = HPG * HEAD_DIM
    for t, (g_ref, z_ref, o_ref) in enumerate(((ga_ref, za_ref, oa_ref), (gb_ref, zb_ref, ob_ref))):
        gates = g_ref[0]
        g_hi = gates.astype(BF16)
        g_lo = (gates - g_hi.astype(F32)).astype(BF16)
        g_exp = _dot(jnp.concatenate([g_hi, g_lo], axis=1), expand_ref[0])
        outs = []
        for pair in range(n_pair):
            a_sel = acc_ref[t, pair]
            a_win = acc_win[t][pair]
            lo = pair * LANES
            outs.append(g_exp[:, lo:lo + LANES] * o_cmp[t][pair]
                        + g_exp[:, width + lo:width + lo + LANES] * (a_sel[:, :LANES] / a_sel[:, LANES:])
                        + g_exp[:, 2 * width + lo:2 * width + lo + LANES] * (a_win[:, :LANES] / a_win[:, LANES:]))
        o_ref[0] = (jnp.concatenate(outs, axis=1) * _silu(z_ref[0])).astype(BF16)


def _attention_pairs(slope_table, q, cmp_kv, kx, vx, gates, z):
    B, S, _ = q.shape
    tq = ATT_TQ
    n_tiles = S // tq
    n_half = n_tiles // 2
    n_cmp_pad = cmp_kv.shape[3]
    expand = _gate_expansion()
    width = HPG * HEAD_DIM

    def even(b, g, i, sl):
        return (b, 2 * i, g)

    def odd(b, g, i, sl):
        return (b, n_tiles - 1 - 2 * i, g)

    grid_spec = pltpu.PrefetchScalarGridSpec(
        num_scalar_prefetch=1,
        grid=(B, N_KV, n_half),
        in_specs=[
            pl.BlockSpec((1, tq, width), even),
            pl.BlockSpec((1, tq, width), odd),
            pl.BlockSpec((1, 2, 1, n_cmp_pad, HEAD_DIM), lambda b, g, i, sl: (b, 0, g, 0, 0)),
            pl.BlockSpec((1, 2, 1, S, LANES), lambda b, g, i, sl: (b, 0, g, 0, 0)),
            pl.BlockSpec((1, 4, 1, S, 2 * LANES), lambda b, g, i, sl: (b, 0, g, 0, 0)),
            pl.BlockSpec((1, tq, GATE_LANES), lambda b, g, i, sl: (b, 2 * i, 0)),
            pl.BlockSpec((1, tq, GATE_LANES), lambda b, g, i, sl: (b, n_tiles - 1 - 2 * i, 0)),
            pl.BlockSpec((1, tq, width), even),
            pl.BlockSpec((1, tq, width), odd),
            pl.BlockSpec((1,) + expand.shape[1:], lambda b, g, i, sl: (g, 0, 0)),
        ],
        out_specs=[
            pl.BlockSpec((1, tq, width), lambda b, g, i, sl: (b, i, g)),
            pl.BlockSpec((1, tq, width), lambda b, g, i, sl: (b, n_half - 1 - i, g)),
        ],
        scratch_shapes=[
            pltpu.VMEM((2, HPG, tq, LANES), BF16),
            pltpu.VMEM((2, HPG, tq, LANES), F32),
            pltpu.VMEM((2, HPG // 2, tq, 2 * LANES), F32),
        ],
    )
    half = jax.ShapeDtypeStruct((B, S // 2, D_ATT), BF16)
    return pl.pallas_call(
        _attn_pair_kernel,
        grid_spec=grid_spec,
        out_shape=[half, half],
        compiler_params=pltpu.CompilerParams(
            dimension_semantics=("parallel", "parallel", "arbitrary"),
            vmem_limit_bytes=VMEM_LIMIT_BYTES),
        name="nsa_attention",
    )(slope_table, q, q, cmp_kv, kx, vx, gates, gates, z, z, expand)


def _nsa_out_kernel(o_even_ref, o_odd_ref, x_ref, w_ref, g_ref, b_ref, out_ref):
    n_tile, tq, d = o_even_ref.shape
    tiles = [ref[i] for i in range(n_tile) for ref in (o_even_ref, o_odd_ref)]
    for i, gated in enumerate(tiles):
        rows = pl.ds(i * tq, tq)
        y = _dot(gated, w_ref[...])
        out_ref[0, rows, :] = _layer_norm(DN_ALPHA * x_ref[0, rows, :] + y, g_ref[...], b_ref[...])


def _nsa_out(o_even, o_odd, x, w_out, ln_g, ln_b):
    B, S, D = x.shape
    tm, tq = PROJ_TM, ATT_TQ
    n_tile = tm // (2 * tq)
    assert tm == n_tile * 2 * tq
    tile = pl.BlockSpec((1, tm, D), lambda b, s: (b, s, 0))
    o_tile = pl.BlockSpec((n_tile, tq, D), lambda b, s: (b * (S // tm) + s, 0, 0))
    o_even = o_even.reshape(B * S // (2 * tq), tq, D)
    o_odd = o_odd.reshape(B * S // (2 * tq), tq, D)
    return pl.pallas_call(
        _nsa_out_kernel,
        grid=(B, S // tm),
        in_specs=[o_tile, o_tile, tile,
                  _const_spec(w_out.shape), _const_spec(ln_g.shape), _const_spec(ln_b.shape)],
        out_specs=tile,
        out_shape=jax.ShapeDtypeStruct((B, S, D), F32),
        compiler_params=pltpu.CompilerParams(
            dimension_semantics=("parallel", "parallel"),
            vmem_limit_bytes=VMEM_LIMIT_BYTES),
        name="nsa_out",
    )(o_even, o_odd, x, w_out, ln_g, ln_b)


def _slope_table():
    h = jnp.arange(1, N_HEADS + 1, dtype=F32)
    slope = (2.0 ** (-8.0 * h / N_HEADS)) * LOG2E
    pieces, rest = [], slope
    for _ in range(N_SLOPE_PIECES):
        piece = rest.astype(BF16).astype(F32)
        pieces.append(piece)
        rest = rest - piece
    cols = [SEL_BLOCK * p for p in pieces] + pieces + [slope]
    cols += [jnp.zeros_like(slope)] * (SLOPE_ROW - len(cols))
    return jnp.stack(cols, axis=1).reshape(-1)


def kernel(x, ln_g, ln_b, pool_w_in, pool_w_grp, pool_scale, pool_w_out, nsa_w_in, nsa_cmp_pos_k, nsa_cmp_w1_k,
           nsa_cmp_w2_k, nsa_cmp_pos_v, nsa_cmp_w1_v, nsa_cmp_w2_v, nsa_w_out):
    seq = x.shape[1]
    assert seq % max(POOL_TM, PROJ_TM, 2 * ATT_TQ) == 0 and x.shape[2] == D_MODEL
    assert seq >= WINDOW + ATT_TQ and WINDOW % ATT_TQ == 0 and ATT_TQ % SEL_BLOCK == 0
    assert seq // SEL_BLOCK == EXT_LANES - EXT_ONEHOT

    x = _pool_layer(x, pool_w_in[0].astype(BF16), pool_w_grp[0].astype(BF16), pool_scale[0][None, :],
                    pool_w_out[0].astype(BF16), ln_g[0][None, :], ln_b[0][None, :])

    w = nsa_w_in[0]
    kv_lo, z_lo, gl_lo = D_ATT, D_ATT + 6 * D_KV, 2 * D_ATT + 6 * D_KV
    wq = w[:, :kv_lo].astype(BF16)
    wkv = w[:, kv_lo:z_lo].astype(BF16)
    wz = w[:, z_lo:gl_lo].astype(BF16)
    wgl = jnp.pad(w[:, gl_lo:], ((0, 0), (0, GATE_LANES - N_BRANCH * N_HEADS))).astype(BF16)
    q, cmp_in, kx, vx, z, gates = _nsa_inproj(x, wq, wkv, wz, wgl)

    half = CMP_BLOCK // 2
    pos = jnp.stack([nsa_cmp_pos_k[0], nsa_cmp_pos_v[0]]).reshape(2, 2, half * HEAD_DIM)
    w1 = jnp.stack([nsa_cmp_w1_k[0], nsa_cmp_w1_v[0]]).astype(BF16)
    w2 = jnp.stack([nsa_cmp_w2_k[0], nsa_cmp_w2_v[0]]).astype(BF16)
    cmp_kv = _compress(cmp_in, pos, w1, w2)

    o_even, o_odd = _attention_pairs(_slope_table(), q, cmp_kv, kx, vx, gates, z)
    return _nsa_out(o_even, o_odd, x, nsa_w_out[0].astype(BF16), ln_g[1][None, :], ln_b[1][None, :])
```

```python
import math

import jax
import jax.numpy as jnp
from jax import lax
from jax.experimental import pallas as pl
from jax.experimental.pallas import tpu as pltpu

F32 = jnp.float32
BF16 = jnp.bfloat16

D_MODEL = 1024
DEPTH = 2
D_POOL = 2 * D_MODEL
POOL_WINDOWS = (2, 4, 8, 16)
POOL_GROUP = D_POOL // len(POOL_WINDOWS)
HEAD_DIM = 64
N_HEADS = D_MODEL // HEAD_DIM
N_KV = 4
HPG = N_HEADS // N_KV
D_ATT = N_HEADS * HEAD_DIM
D_KV = N_KV * HEAD_DIM
N_BRANCH = 3
CMP_BLOCK = 32
CMP_STRIDE = 16
SEL_BLOCK = 64
SEL_TOPN = 8
WINDOW = 256
DN_ALPHA = (2.0 * DEPTH) ** 0.25
LN_EPS = 1e-5
NEG_INF = -1e30
FORCE_SCORE = 1e9
LOG2E = math.log2(math.e)

LANES = 128
SUBLANES = 8
VMEM_LIMIT_BYTES = 56 * 1024 * 1024

POOL_TM = 512
POOL_HALO = 32
PROJ_TM = 512
ATT_TQ = 256
ATT_KB = 512
ATT_PAIRS = 2
GATE_LANES = LANES

N_SLOPE_PIECES = 3
EXT_ONEHOT = 32
SLOPE_ROW = 8
EXT_LANES = LANES - HEAD_DIM


def _dot(a, b):
    return jnp.dot(a, b, preferred_element_type=F32)


def _dot_nt(a, b):
    return lax.dot_general(a, b, (((1,), (1,)), ((), ())), preferred_element_type=F32)


def _div_pow2(x, n):
    assert n & (n - 1) == 0
    return lax.shift_right_logical(x, n.bit_length() - 1)


def _silu(x):
    return x * jax.nn.sigmoid(x)


def _layer_norm(r, g, b):
    mu = jnp.mean(r, axis=-1, keepdims=True)
    d = r - mu
    var = jnp.mean(d * d, axis=-1, keepdims=True)
    return d * lax.rsqrt(var + LN_EPS) * g + b


def _const_spec(shape):
    n = len(shape)
    return pl.BlockSpec(shape, lambda *_: (0,) * n, pipeline_mode=pl.Buffered(1))


def _pool_layer_kernel(x_ref, w_in_ref, w_grp_ref, scale_ref, w_out_ref, g_ref, b_ref,
                       o_ref, carry_ref, work_a_ref, work_b_ref):
    tm = x_ref.shape[1]
    work = (work_a_ref, work_b_ref)
    s = pl.program_id(1)

    @pl.when(s == 0)
    def _():
        carry_ref[...] = jnp.zeros_like(carry_ref)

    xt = x_ref[0]
    xb = xt.astype(BF16)
    t = s * tm + lax.broadcasted_iota(jnp.int32, (tm, 1), 0)
    y = jnp.zeros((tm, D_MODEL), F32)
    for g, w in enumerate(POOL_WINDOWS):
        lo = g * POOL_GROUP
        u = _dot(xb, w_in_ref[:, lo:lo + POOL_GROUP])
        z = _dot(xb, w_in_ref[:, D_POOL + lo:D_POOL + lo + POOL_GROUP])
        work[0][0:POOL_HALO, :] = carry_ref[g]
        work[0][POOL_HALO:, :] = u
        carry_ref[g] = u[tm - POOL_HALO:, :]
        n_steps = w.bit_length() - 1
        assert w == 1 << n_steps and SUBLANES * n_steps <= POOL_HALO
        src = 0
        for j in range(n_steps - 1):
            shift, lo_row = 1 << j, SUBLANES * (j + 1)
            n = POOL_HALO + tm - lo_row
            work[1 - src][lo_row:, :] = work[src][lo_row:, :] + work[src][lo_row - shift:lo_row - shift + n, :]
            src = 1 - src
        shift = w // 2
        tot = work[src][POOL_HALO:, :] + work[src][POOL_HALO - shift:POOL_HALO - shift + tm, :]
        inv_cnt = 1.0 / jnp.minimum(t + 1, w).astype(F32)
        pooled = tot * inv_cnt - u
        m = _dot(pooled.astype(BF16), w_grp_ref[g]) * scale_ref[:, lo:lo + POOL_GROUP]
        gated = (m * _silu(z)).astype(BF16)
        if g + 1 < len(POOL_WINDOWS):
            y = y + _dot(gated, w_out_ref[lo:lo + POOL_GROUP, :])
        else:
            for r0 in range(0, tm, tm // 2):
                rows = slice(r0, r0 + tm // 2)
                y_rows = y[rows] + _dot(gated[rows], w_out_ref[lo:lo + POOL_GROUP, :])
                o_ref[0, rows, :] = _layer_norm(DN_ALPHA * xt[rows] + y_rows, g_ref[...], b_ref[...])


def _pool_layer(x, w_in, w_grp, scale, w_out, ln_g, ln_b):
    B, S, D = x.shape
    tm = POOL_TM
    return pl.pallas_call(
        _pool_layer_kernel,
        grid=(B, S // tm),
        in_specs=[
            pl.BlockSpec((1, tm, D), lambda b, s: (b, s, 0)),
            _const_spec(w_in.shape),
            _const_spec(w_grp.shape),
            _const_spec(scale.shape),
            _const_spec(w_out.shape),
            _const_spec(ln_g.shape),
            _const_spec(ln_b.shape),
        ],
        out_specs=pl.BlockSpec((1, tm, D), lambda b, s: (b, s, 0)),
        out_shape=jax.ShapeDtypeStruct((B, S, D), F32),
        scratch_shapes=[
            pltpu.VMEM((len(POOL_WINDOWS), POOL_HALO, POOL_GROUP), F32),
            pltpu.VMEM((POOL_HALO + tm, POOL_GROUP), F32),
            pltpu.VMEM((POOL_HALO + tm, POOL_GROUP), F32),
        ],
        compiler_params=pltpu.CompilerParams(
            dimension_semantics=("parallel", "arbitrary"),
            vmem_limit_bytes=VMEM_LIMIT_BYTES),
        name="pool_layer",
    )(x, w_in, w_grp, scale, w_out, ln_g, ln_b)


def _nsa_inproj_kernel(x_ref, wq_ref, wkv_ref, wz_ref, wgl_ref,
                       q_ref, cmp_ref, kx_ref, vx_ref, z_ref, gate_ref):
    tm = x_ref.shape[1]
    xb = x_ref[0].astype(BF16)
    q_ref[0] = (_dot(xb, wq_ref[...]) * (HEAD_DIM ** -0.5 * LOG2E)).astype(BF16)
    kv = _dot(xb, wkv_ref[...])
    for a in range(2):
        for g in range(N_KV):
            lo = a * D_KV + g * HEAD_DIM
            cmp_ref[0, a, g] = kv[:, lo:lo + HEAD_DIM]

    lane = lax.broadcasted_iota(jnp.int32, (tm, LANES), 1)
    ext = lane - HEAD_DIM
    low = ext < 0
    pos = pl.program_id(1) * tm + lax.broadcasted_iota(jnp.int32, (tm, LANES), 0)
    blk = _div_pow2(pos, SEL_BLOCK)
    alibi = jnp.where(ext < N_SLOPE_PIECES, blk, jnp.where(ext < 2 * N_SLOPE_PIECES, pos - blk * SEL_BLOCK, 0))
    k_win_ext = alibi.astype(F32)
    k_sel_ext = jnp.where(ext >= EXT_ONEHOT, jnp.where(ext - EXT_ONEHOT == blk, 1.0, 0.0), k_win_ext)
    ones_a = jnp.where(low, 1.0, 0.0).astype(BF16)
    ones_b = jnp.where(low, 0.0, 1.0).astype(BF16)
    for a in range(2, 6):
        for pair in range(N_KV // 2):
            lo = a * D_KV + pair * LANES
            both = kv[:, lo:lo + LANES]
            swapped = pltpu.roll(both, HEAD_DIM, axis=1)
            for g, (at_low, at_high) in ((2 * pair, (both, swapped)), (2 * pair + 1, (swapped, both))):
                if a % 2 == 0:
                    k_ext = k_sel_ext if a == 2 else k_win_ext
                    kx_ref[0, (a - 2) // 2, g] = jnp.where(low, at_low, k_ext).astype(BF16)
                else:
                    va = jnp.where(low, at_low, 0.0).astype(BF16)
                    vb = jnp.where(low, 0.0, at_high).astype(BF16)
                    vx_ref[0, a - 3, g] = jnp.concatenate([va, ones_a], axis=1)
                    vx_ref[0, a - 2, g] = jnp.concatenate([vb, ones_b], axis=1)

    z_ref[0] = _dot(xb, wz_ref[...])
    gate_ref[0] = jax.nn.sigmoid(_dot(xb, wgl_ref[...]))


def _nsa_inproj(x, wq, wkv, wz, wgl):
    B, S, D = x.shape
    tm = PROJ_TM
    return pl.pallas_call(
        _nsa_inproj_kernel,
        grid=(B, S // tm),
        in_specs=[
            pl.BlockSpec((1, tm, D), lambda b, s: (b, s, 0)),
            _const_spec(wq.shape),
            _const_spec(wkv.shape),
            _const_spec(wz.shape),
            _const_spec(wgl.shape),
        ],
        out_specs=[
            pl.BlockSpec((1, tm, D_ATT), lambda b, s: (b, s, 0)),
            pl.BlockSpec((1, 2, N_KV, tm, HEAD_DIM), lambda b, s: (b, 0, 0, s, 0)),
            pl.BlockSpec((1, 2, N_KV, tm, LANES), lambda b, s: (b, 0, 0, s, 0)),
            pl.BlockSpec((1, 4, N_KV, tm, 2 * LANES), lambda b, s: (b, 0, 0, s, 0)),
            pl.BlockSpec((1, tm, D_ATT), lambda b, s: (b, s, 0)),
            pl.BlockSpec((1, tm, GATE_LANES), lambda b, s: (b, s, 0)),
        ],
        out_shape=[
            jax.ShapeDtypeStruct((B, S, D_ATT), BF16),
            jax.ShapeDtypeStruct((B, 2, N_KV, S, HEAD_DIM), F32),
            jax.ShapeDtypeStruct((B, 2, N_KV, S, LANES), BF16),
            jax.ShapeDtypeStruct((B, 4, N_KV, S, 2 * LANES), BF16),
            jax.ShapeDtypeStruct((B, S, D_ATT), F32),
            jax.ShapeDtypeStruct((B, S, GATE_LANES), F32),
        ],
        compiler_params=pltpu.CompilerParams(
            dimension_semantics=("parallel", "parallel"),
            vmem_limit_bytes=VMEM_LIMIT_BYTES),
        name="nsa_inproj",
    )(x, wq, wkv, wz, wgl)


def _compress_kernel(c_ref, pos_ref, w1_ref, w2_ref, o_ref):
    n_grp, seq = c_ref.shape[2:4]
    n_row = seq // CMP_STRIDE
    width = CMP_STRIDE * HEAD_DIM
    groups = []
    for g in range(n_grp):
        pieces = [c_ref[0, 0, g, pl.ds(l, n_row, stride=CMP_STRIDE), :] for l in range(CMP_STRIDE)]
        groups.append(jnp.concatenate(pieces, axis=1))
    rows = jnp.concatenate(groups, axis=0)
    nxt = pltpu.roll(rows, shift=n_grp * n_row - 1, axis=0)
    first = (rows + pos_ref[0, 0:1, :]).astype(BF16)
    second = (nxt + pos_ref[0, 1:2, :]).astype(BF16)
    h = _dot(first, w1_ref[0, :width, :]) + _dot(second, w1_ref[0, width:, :])
    out = _dot(_silu(h).astype(BF16), w2_ref[0])
    o_ref[0, 0] = out.reshape(n_grp, n_row, HEAD_DIM).astype(BF16)


def _compress(cmp_in, pos, w1, w2):
    B = cmp_in.shape[0]
    S = cmp_in.shape[3]
    n_row = S // CMP_STRIDE
    width = CMP_STRIDE * HEAD_DIM
    return pl.pallas_call(
        _compress_kernel,
        grid=(2, B),
        in_specs=[
            pl.BlockSpec((1, 1, N_KV, S, HEAD_DIM), lambda a, b: (b, a, 0, 0, 0)),
            pl.BlockSpec((1, 2, width), lambda a, b: (a, 0, 0)),
            pl.BlockSpec((1,) + w1.shape[1:], lambda a, b: (a, 0, 0)),
            pl.BlockSpec((1,) + w2.shape[1:], lambda a, b: (a, 0, 0)),
        ],
        out_specs=pl.BlockSpec((1, 1, N_KV, n_row, HEAD_DIM), lambda a, b: (b, a, 0, 0, 0)),
        out_shape=jax.ShapeDtypeStruct((B, 2, N_KV, n_row, HEAD_DIM), BF16),
        compiler_params=pltpu.CompilerParams(
            dimension_semantics=("parallel", "parallel"),
            vmem_limit_bytes=VMEM_LIMIT_BYTES),
        name="nsa_compress",
    )(cmp_in, pos, w1, w2)


def _gate_expansion():
    r = jnp.arange(2 * GATE_LANES)[None, :, None] % GATE_LANES
    c = jnp.arange(N_BRANCH * HPG * HEAD_DIM)[None, None, :]
    g = jnp.arange(N_KV)[:, None, None]
    br, h = c // (HPG * HEAD_DIM), (c // HEAD_DIM) % HPG
    return (r == g * HPG * N_BRANCH + h * N_BRANCH + br).astype(BF16)


def _loop_chunks(tile, tq, kb):
    return (tile * tq) // kb


def _attn_pair_kernel(slope_ref, *refs):
    n_slot = 2 * ATT_PAIRS
    q_refs = refs[:n_slot]
    cmp_ref, kx_ref, vx_ref = refs[n_slot:n_slot + 3]
    g_refs = refs[n_slot + 3:2 * n_slot + 3]
    z_refs = refs[2 * n_slot + 3:3 * n_slot + 3]
    expand_ref = refs[3 * n_slot + 3]
    o_even_ref, o_odd_ref = refs[3 * n_slot + 4:3 * n_slot + 6]
    qs_ref, m_ref, acc_ref = refs[3 * n_slot + 6:]
    tq = q_refs[0].shape[1]
    n_cmp_pad = cmp_ref.shape[3]
    seq = kx_ref.shape[3]
    n_sel = seq // SEL_BLOCK
    n_tiles = seq // tq
    kb = ATT_KB
    n_pair = HPG // 2
    grp = pl.program_id(1)
    pstep = pl.program_id(2)
    assert kb == 2 * tq and kb - tq == WINDOW and EXT_LANES - EXT_ONEHOT == n_sel and n_tiles % 2 == 0
    n_static = _loop_chunks(0, tq, kb) + _loop_chunks(n_tiles - 1, tq, kb)
    assert all(_loop_chunks(2 * i, tq, kb) + _loop_chunks(n_tiles - 1 - 2 * i, tq, kb) == n_static
               for i in range(n_tiles // 2))

    lane = lax.broadcasted_iota(jnp.int32, (1, LANES), 1)
    low = lane < HEAD_DIM
    first_head = jnp.bitwise_and(lax.broadcasted_iota(jnp.int32, (1, 2 * LANES), 1), HEAD_DIM) == 0

    k_cmp = cmp_ref[0, 0, 0].astype(F32)
    v_cmp = cmp_ref[0, 1, 0].astype(F32)
    zero_half = jnp.zeros_like(v_cmp)
    k_cmp_x = jnp.concatenate([k_cmp, zero_half], axis=1).astype(BF16)
    v_cmp_x = (jnp.concatenate([v_cmp, zero_half], axis=1).astype(BF16),
               jnp.concatenate([zero_half, v_cmp], axis=1).astype(BF16))
    j_ov = lax.broadcasted_iota(jnp.int32, (n_sel, n_cmp_pad), 0) * SEL_BLOCK
    c_ov = lax.broadcasted_iota(jnp.int32, (n_sel, n_cmp_pad), 1) * CMP_STRIDE
    overlap_t = jnp.where(c_ov < j_ov + SEL_BLOCK, jnp.where(c_ov + CMP_BLOCK > j_ov, 1.0, 0.0), 0.0).astype(BF16)
    slope_rows, slopes = [], []
    for h in range(HPG):
        base = (grp * HPG + h) * SLOPE_ROW
        row = jnp.zeros((1, LANES), F32)
        for j in range(2 * N_SLOPE_PIECES):
            row = jnp.where(lane == HEAD_DIM + j, slope_ref[base + j], row)
        slope_rows.append(row)
        slopes.append(slope_ref[base + 2 * N_SLOPE_PIECES])
    c_row = lax.broadcasted_iota(jnp.int32, (1, n_cmp_pad), 1)
    j_blk = lax.broadcasted_iota(jnp.int32, (n_sel, tq), 0)
    j_loc = lax.broadcasted_iota(jnp.int32, (SUBLANES, tq), 0)
    n_tile = n_sel // SUBLANES
    top_n = float(min(SEL_TOPN, n_sel))

    def tile_lanes(x, n):
        return jnp.concatenate([x] * n, axis=1)

    def prepare(q_ref, q0):
        t_col = q0 + lax.broadcasted_iota(jnp.int32, (tq, 1), 0)
        qf = q_ref[0].astype(F32)
        q_src = []
        for pair in range(n_pair):
            both = qf[:, pair * LANES:(pair + 1) * LANES]
            q_src += [both, pltpu.roll(both, HEAD_DIM, axis=1)]
        q_win = [jnp.where(low, q_src[h], slope_rows[h]).astype(BF16) for h in range(HPG)]

        dist_c = t_col - (c_row * CMP_STRIDE + (CMP_BLOCK - 1))
        valid_c = dist_c >= 0
        dist_cf = dist_c.astype(F32)
        p_sum = jnp.zeros((tq, n_cmp_pad), F32)
        o_cmp = []
        for pair in range(n_pair):
            o_pair = None
            for hh in range(2):
                h = 2 * pair + hh
                s = _dot_nt(q_src[h].astype(BF16), k_cmp_x) - slopes[h] * dist_cf
                s = jnp.where(valid_c, s, NEG_INF)
                m = jnp.max(s, axis=-1, keepdims=True)
                p = jnp.where(valid_c, jnp.exp2(s - m), 0.0)
                l = jnp.sum(p, axis=-1, keepdims=True)
                p = p * jnp.where(l > 0.0, 1.0 / l, 0.0)
                p_sum = p_sum + p
                d = _dot(p.astype(BF16), v_cmp_x[hh])
                o_pair = d if o_pair is None else o_pair + d
            o_cmp.append(o_pair)

        p_hi = p_sum.astype(BF16)
        p_lo = (p_sum - p_hi.astype(F32)).astype(BF16)
        imp = _dot_nt(overlap_t, p_hi) + _dot_nt(overlap_t, p_lo)
        cur = _div_pow2(q0 + lax.broadcasted_iota(jnp.int32, (n_sel, tq), 1), SEL_BLOCK)
        forced = (j_blk == 0) | (j_blk == cur) | (j_blk == cur - 1)
        imp = jnp.where(forced, FORCE_SCORE, jnp.where(j_blk > cur, NEG_INF, imp))
        tiles = [imp[v * SUBLANES:(v + 1) * SUBLANES, :] for v in range(n_tile)]
        rank = [jnp.zeros((SUBLANES, tq), F32) for _ in range(n_tile)]
        for i in range(n_sel):
            v, i_loc = divmod(i, SUBLANES)
            row = tiles[v][i_loc:i_loc + 1, :]
            for w in range(n_tile):
                if w > v:
                    ahead = jnp.where(row >= tiles[w], 1.0, 0.0)
                elif w < v:
                    ahead = jnp.where(row > tiles[w], 1.0, 0.0)
                else:
                    ahead = jnp.where(j_loc > i_loc, jnp.where(row >= tiles[w], 1.0, 0.0),
                                      jnp.where(row > tiles[w], 1.0, 0.0))
                rank[w] = rank[w] + ahead
        mask_rows = [jnp.zeros((HEAD_DIM + EXT_ONEHOT, tq), F32)] + [jnp.where(r < top_n, 0.0, NEG_INF) for r in rank]
        mask_ext = jnp.transpose(jnp.concatenate(mask_rows, axis=0))
        q_aug = [jnp.where(low, q_src[h], mask_ext + slope_rows[h]).astype(BF16) for h in range(HPG)]
        return t_col, o_cmp, q_aug, q_win

    def chunk_init(queries, which_k, v_base, k0, n_keys, bias):
        k_c = kx_ref[0, which_k, 0, pl.ds(k0, n_keys), :]
        ms, accs = [], []
        for pair in range(n_pair):
            pv = None
            for hh in range(2):
                s = _dot_nt(queries[2 * pair + hh], k_c) + bias
                m = jnp.max(s, axis=-1, keepdims=True)
                p = jnp.exp2(s - m).astype(BF16)
                d = _dot(p, vx_ref[0, v_base + hh, 0, pl.ds(k0, n_keys), :])
                pv = d if pv is None else pv + d
                ms.append(m)
            accs.append(pv)
        return ms, accs

    tile_of = []
    for k in range(ATT_PAIRS):
        even_tile = 2 * (ATT_PAIRS * pstep + k)
        tile_of += [even_tile, n_tiles - 1 - even_tile]
    q0s = [tile_of[t] * tq for t in range(n_slot)]
    n_loop = [_div_pow2(q0, kb) for q0 in q0s]
    k0s = [pl.multiple_of(jnp.maximum(q0 - WINDOW, 0), LANES) for q0 in q0s]
    prepared = [prepare(q_refs[t], q0s[t]) for t in range(n_slot)]
    o_cmp = [p[1] for p in prepared]
    acc_win = []
    for t in range(n_slot):
        t_col, _, _, q_win = prepared[t]
        pos = k0s[t] + lax.broadcasted_iota(jnp.int32, (1, kb), 1)
        win_bias = jnp.where(pos <= t_col, jnp.where(pos > t_col - WINDOW, 0.0, NEG_INF), NEG_INF)
        acc_win.append(chunk_init(q_win, 1, 2, k0s[t], kb, win_bias)[1])
    for t in range(n_slot):
        t_col, _, q_aug, _ = prepared[t]
        if t % 2 == 0:
            s0, n_keys = pl.multiple_of(q0s[t], LANES), tq
        else:
            s0, n_keys = k0s[t], kb
        pos_s = s0 + lax.broadcasted_iota(jnp.int32, (1, n_keys), 1)
        sel_bias = jnp.where(pos_s <= t_col, 0.0, NEG_INF)
        m_sel, acc_sel = chunk_init(q_aug, 0, 0, s0, n_keys, sel_bias)
        for h in range(HPG):
            m_ref[t, h] = jnp.broadcast_to(m_sel[h], (tq, LANES))
            qs_ref[t, h] = q_aug[h]
        for pair in range(n_pair):
            acc_ref[t, pair] = acc_sel[pair]

    for k, j in [(k, j) for k in range(ATT_PAIRS) for j in range(n_static)]:
        mine = j < n_loop[2 * k]
        t = jnp.where(mine, 2 * k, 2 * k + 1)
        kc0 = pl.multiple_of(jnp.where(mine, j, j - n_loop[2 * k]) * kb, kb)
        k_c = kx_ref[0, 0, 0, pl.ds(kc0, kb), :]
        for pair in range(n_pair):
            pv, scale = None, []
            for hh in range(2):
                h = 2 * pair + hh
                s = _dot_nt(qs_ref[t, h], k_c)
                m_old = m_ref[t, h]
                m_new = jnp.maximum(m_old, jnp.max(s, axis=-1, keepdims=True))
                m_ref[t, h] = m_new
                scale.append(tile_lanes(jnp.exp2(m_old - m_new), 2))
                p = jnp.exp2(s - tile_lanes(m_new, kb // LANES)).astype(BF16)
                d = _dot(p, vx_ref[0, hh, 0, pl.ds(kc0, kb), :])
                pv = d if pv is None else pv + d
            acc_ref[t, pair] = jnp.where(first_head, scale[0], scale[1]) * acc_ref[t, pair] + pv

    width = HPG * HEAD_DIM
    for t, (g_ref, z_ref) in enumerate(zip(g_refs, z_refs)):
        gates = g_ref[0]
        g_hi = gates.astype(BF16)
        g_lo = (gates - g_hi.astype(F32)).astype(BF16)
        g_exp = _dot(jnp.concatenate([g_hi, g_lo], axis=1), expand_ref[0])
        outs = []
        for pair in range(n_pair):
            a_sel = acc_ref[t, pair]
            a_win = acc_win[t][pair]
            lo = pair * LANES
            outs.append(g_exp[:, lo:lo + LANES] * o_cmp[t][pair]
                        + g_exp[:, width + lo:width + lo + LANES] * (a_sel[:, :LANES] / a_sel[:, LANES:])
                        + g_exp[:, 2 * width + lo:2 * width + lo + LANES] * (a_win[:, :LANES] / a_win[:, LANES:]))
        gated = (jnp.concatenate(outs, axis=1) * _silu(z_ref[0])).astype(BF16)
        if t % 2 == 0:
            o_even_ref[0, pl.ds((t // 2) * tq, tq), :] = gated
        else:
            o_odd_ref[0, pl.ds((ATT_PAIRS - 1 - t // 2) * tq, tq), :] = gated


def _attention_pairs(slope_table, q, cmp_kv, kx, vx, gates, z):
    B, S, _ = q.shape
    tq = ATT_TQ
    n_tiles = S // tq
    n_half = n_tiles // 2
    n_cmp_pad = cmp_kv.shape[3]
    expand = _gate_expansion()
    width = HPG * HEAD_DIM

    n_slot = 2 * ATT_PAIRS
    n_step = n_half // ATT_PAIRS
    assert n_half == n_step * ATT_PAIRS

    def tile(slot, i):
        even_tile = 2 * (ATT_PAIRS * i + slot // 2)
        return even_tile if slot % 2 == 0 else n_tiles - 1 - even_tile

    def per_group(slot):
        return pl.BlockSpec((1, tq, width), lambda b, g, i, sl: (b, tile(slot, i), g))

    def all_groups(slot):
        return pl.BlockSpec((1, tq, GATE_LANES), lambda b, g, i, sl: (b, tile(slot, i), 0))

    grid_spec = pltpu.PrefetchScalarGridSpec(
        num_scalar_prefetch=1,
        grid=(B, N_KV, n_step),
        in_specs=[per_group(slot) for slot in range(n_slot)] + [
            pl.BlockSpec((1, 2, 1, n_cmp_pad, HEAD_DIM), lambda b, g, i, sl: (b, 0, g, 0, 0)),
            pl.BlockSpec((1, 2, 1, S, LANES), lambda b, g, i, sl: (b, 0, g, 0, 0)),
            pl.BlockSpec((1, 4, 1, S, 2 * LANES), lambda b, g, i, sl: (b, 0, g, 0, 0)),
        ] + [all_groups(slot) for slot in range(n_slot)] + [per_group(slot) for slot in range(n_slot)] + [
            pl.BlockSpec((1,) + expand.shape[1:], lambda b, g, i, sl: (g, 0, 0)),
        ],
        out_specs=[
            pl.BlockSpec((1, ATT_PAIRS * tq, width), lambda b, g, i, sl: (b, i, g)),
            pl.BlockSpec((1, ATT_PAIRS * tq, width), lambda b, g, i, sl: (b, n_step - 1 - i, g)),
        ],
        scratch_shapes=[
            pltpu.VMEM((n_slot, HPG, tq, LANES), BF16),
            pltpu.VMEM((n_slot, HPG, tq, LANES), F32),
            pltpu.VMEM((n_slot, HPG // 2, tq, 2 * LANES), F32),
        ],
    )
    half = jax.ShapeDtypeStruct((B, S // 2, D_ATT), BF16)
    return pl.pallas_call(
        _attn_pair_kernel,
        grid_spec=grid_spec,
        out_shape=[half, half],
        compiler_params=pltpu.CompilerParams(
            dimension_semantics=("parallel", "parallel", "arbitrary"),
            vmem_limit_bytes=VMEM_LIMIT_BYTES),
        name="nsa_attention",
    )(slope_table, *[q] * n_slot, cmp_kv, kx, vx, *[gates] * n_slot, *[z] * n_slot, expand)


def _nsa_out_kernel(o_even_ref, o_odd_ref, x_ref, w_ref, g_ref, b_ref, out_ref):
    n_tile, tq, d = o_even_ref.shape
    tiles = [ref[i] for i in range(n_tile) for ref in (o_even_ref, o_odd_ref)]
    for i, gated in enumerate(tiles):
        rows = pl.ds(i * tq, tq)
        y = _dot(gated, w_ref[...])
        out_ref[0, rows, :] = _layer_norm(DN_ALPHA * x_ref[0, rows, :] + y, g_ref[...], b_ref[...])


def _nsa_out(o_even, o_odd, x, w_out, ln_g, ln_b):
    B, S, D = x.shape
    tm, tq = PROJ_TM, ATT_TQ
    n_tile = tm // (2 * tq)
    assert tm == n_tile * 2 * tq
    tile = pl.BlockSpec((1, tm, D), lambda b, s: (b, s, 0))
    o_tile = pl.BlockSpec((n_tile, tq, D), lambda b, s: (b * (S // tm) + s, 0, 0))
    o_even = o_even.reshape(B * S // (2 * tq), tq, D)
    o_odd = o_odd.reshape(B * S // (2 * tq), tq, D)
    return pl.pallas_call(
        _nsa_out_kernel,
        grid=(B, S // tm),
        in_specs=[o_tile, o_tile, tile,
                  _const_spec(w_out.shape), _const_spec(ln_g.shape), _const_spec(ln_b.shape)],
        out_specs=tile,
        out_shape=jax.ShapeDtypeStruct((B, S, D), F32),
        compiler_params=pltpu.CompilerParams(
            dimension_semantics=("parallel", "parallel"),
            vmem_limit_bytes=VMEM_LIMIT_BYTES),
        name="nsa_out",
    )(o_even, o_odd, x, w_out, ln_g, ln_b)


def _slope_table():
    h = jnp.arange(1, N_HEADS + 1, dtype=F32)
    slope = (2.0 ** (-8.0 * h / N_HEADS)) * LOG2E
    pieces, rest = [], slope
    for _ in range(N_SLOPE_PIECES):
        piece = rest.astype(BF16).astype(F32)
        pieces.append(piece)
        rest = rest - piece
    cols = [SEL_BLOCK * p for p in pieces] + pieces + [slope]
    cols += [jnp.zeros_like(slope)] * (SLOPE_ROW - len(cols))
    return jnp.stack(cols, axis=1).reshape(-1)


def kernel(x, ln_g, ln_b, pool_w_in, pool_w_grp, pool_scale, pool_w_out, nsa_w_in, nsa_cmp_pos_k, nsa_cmp_w1_k,
           nsa_cmp_w2_k, nsa_cmp_pos_v, nsa_cmp_w1_v, nsa_cmp_w2_v, nsa_w_out):
    seq = x.shape[1]
    assert seq % max(POOL_TM, PROJ_TM, 2 * ATT_TQ) == 0 and x.shape[2] == D_MODEL
    assert seq >= WINDOW + ATT_TQ and WINDOW % ATT_TQ == 0 and ATT_TQ % SEL_BLOCK == 0
    assert seq // SEL_BLOCK == EXT_LANES - EXT_ONEHOT

    x = _pool_layer(x, pool_w_in[0].astype(BF16), pool_w_grp[0].astype(BF16), pool_scale[0][None, :],
                    pool_w_out[0].astype(BF16), ln_g[0][None, :], ln_b[0][None, :])

    w = nsa_w_in[0]
    kv_lo, z_lo, gl_lo = D_ATT, D_ATT + 6 * D_KV, 2 * D_ATT + 6 * D_KV
    wq = w[:, :kv_lo].astype(BF16)
    wkv = w[:, kv_lo:z_lo].astype(BF16)
    wz = w[:, z_lo:gl_lo].astype(BF16)
    wgl = jnp.pad(w[:, gl_lo:], ((0, 0), (0, GATE_LANES - N_BRANCH * N_HEADS))).astype(BF16)
    q, cmp_in, kx, vx, z, gates = _nsa_inproj(x, wq, wkv, wz, wgl)

    half = CMP_BLOCK // 2
    pos = jnp.stack([nsa_cmp_pos_k[0], nsa_cmp_pos_v[0]]).reshape(2, 2, half * HEAD_DIM)
    w1 = jnp.stack([nsa_cmp_w1_k[0], nsa_cmp_w1_v[0]]).astype(BF16)
    w2 = jnp.stack([nsa_cmp_w2_k[0], nsa_cmp_w2_v[0]]).astype(BF16)
    cmp_kv = _compress(cmp_in, pos, w1, w2)

    o_even, o_odd = _attention_pairs(_slope_table(), q, cmp_kv, kx, vx, gates, z)
    return _nsa_out(o_even, o_odd, x, nsa_w_out[0].astype(BF16), ln_g[1][None, :], ln_b[1][None, :])
```

```python
import math

import jax
import jax.numpy as jnp
from jax import lax
from jax.experimental import pallas as pl
from jax.experimental.pallas import tpu as pltpu

F32 = jnp.float32
BF16 = jnp.bfloat16

D_MODEL = 1024
DEPTH = 2
D_POOL = 2 * D_MODEL
POOL_WINDOWS = (2, 4, 8, 16)
POOL_GROUP = D_POOL // len(POOL_WINDOWS)
HEAD_DIM = 64
N_HEADS = D_MODEL // HEAD_DIM
N_KV = 4
HPG = N_HEADS // N_KV
D_ATT = N_HEADS * HEAD_DIM
D_KV = N_KV * HEAD_DIM
N_BRANCH = 3
CMP_BLOCK = 32
CMP_STRIDE = 16
SEL_BLOCK = 64
SEL_TOPN = 8
WINDOW = 256
DN_ALPHA = (2.0 * DEPTH) ** 0.25
LN_EPS = 1e-5
NEG_INF = -1e30
FORCE_SCORE = 1e9
LOG2E = math.log2(math.e)

LANES = 128
SUBLANES = 8
VMEM_LIMIT_BYTES = 56 * 1024 * 1024

POOL_TM = 512
POOL_HALO = 32
PROJ_TM = 512
ATT_TQ = 256
ATT_KB = 512
ATT_PAIRS = 4
GATE_LANES = LANES

N_SLOPE_PIECES = 3
EXT_ONEHOT = 32
SLOPE_ROW = 8
EXT_LANES = LANES - HEAD_DIM


def _dot(a, b):
    return jnp.dot(a, b, preferred_element_type=F32)


def _dot_nt(a, b):
    return lax.dot_general(a, b, (((1,), (1,)), ((), ())), preferred_element_type=F32)


def _div_pow2(x, n):
    assert n & (n - 1) == 0
    return lax.shift_right_logical(x, n.bit_length() - 1)


def _silu(x):
    return x * jax.nn.sigmoid(x)


def _layer_norm(r, g, b):
    mu = jnp.mean(r, axis=-1, keepdims=True)
    d = r - mu
    var = jnp.mean(d * d, axis=-1, keepdims=True)
    return d * lax.rsqrt(var + LN_EPS) * g + b


def _const_spec(shape):
    n = len(shape)
    return pl.BlockSpec(shape, lambda *_: (0,) * n, pipeline_mode=pl.Buffered(1))


def _pool_layer_kernel(x_ref, w_in_ref, w_grp_ref, scale_ref, w_out_ref, g_ref, b_ref,
                       o_ref, carry_ref, work_a_ref, work_b_ref):
    tm = x_ref.shape[1]
    work = (work_a_ref, work_b_ref)
    s = pl.program_id(1)

    @pl.when(s == 0)
    def _():
        carry_ref[...] = jnp.zeros_like(carry_ref)

    xt = x_ref[0]
    xb = xt.astype(BF16)
    t = s * tm + lax.broadcasted_iota(jnp.int32, (tm, 1), 0)
    y = jnp.zeros((tm, D_MODEL), F32)
    for g, w in enumerate(POOL_WINDOWS):
        lo = g * POOL_GROUP
        u = _dot(xb, w_in_ref[:, lo:lo + POOL_GROUP])
        z = _dot(xb, w_in_ref[:, D_POOL + lo:D_POOL + lo + POOL_GROUP])
        work[0][0:POOL_HALO, :] = carry_ref[g]
        work[0][POOL_HALO:, :] = u
        carry_ref[g] = u[tm - POOL_HALO:, :]
        n_steps = w.bit_length() - 1
        assert w == 1 << n_steps and SUBLANES * n_steps <= POOL_HALO
        src = 0
        for j in range(n_steps - 1):
            shift, lo_row = 1 << j, SUBLANES * (j + 1)
            n = POOL_HALO + tm - lo_row
            work[1 - src][lo_row:, :] = work[src][lo_row:, :] + work[src][lo_row - shift:lo_row - shift + n, :]
            src = 1 - src
        shift = w // 2
        tot = work[src][POOL_HALO:, :] + work[src][POOL_HALO - shift:POOL_HALO - shift + tm, :]
        inv_cnt = 1.0 / jnp.minimum(t + 1, w).astype(F32)
        pooled = tot * inv_cnt - u
        m = _dot(pooled.astype(BF16), w_grp_ref[g]) * scale_ref[:, lo:lo + POOL_GROUP]
        gated = (m * _silu(z)).astype(BF16)
        if g + 1 < len(POOL_WINDOWS):
            y = y + _dot(gated, w_out_ref[lo:lo + POOL_GROUP, :])
        else:
            for r0 in range(0, tm, tm // 2):
                rows = slice(r0, r0 + tm // 2)
                y_rows = y[rows] + _dot(gated[rows], w_out_ref[lo:lo + POOL_GROUP, :])
                o_ref[0, rows, :] = _layer_norm(DN_ALPHA * xt[rows] + y_rows, g_ref[...], b_ref[...])


def _pool_layer(x, w_in, w_grp, scale, w_out, ln_g, ln_b):
    B, S, D = x.shape
    tm = POOL_TM
    return pl.pallas_call(
        _pool_layer_kernel,
        grid=(B, S // tm),
        in_specs=[
            pl.BlockSpec((1, tm, D), lambda b, s: (b, s, 0)),
            _const_spec(w_in.shape),
            _const_spec(w_grp.shape),
            _const_spec(scale.shape),
            _const_spec(w_out.shape),
            _const_spec(ln_g.shape),
            _const_spec(ln_b.shape),
        ],
        out_specs=pl.BlockSpec((1, tm, D), lambda b, s: (b, s, 0)),
        out_shape=jax.ShapeDtypeStruct((B, S, D), F32),
        scratch_shapes=[
            pltpu.VMEM((len(POOL_WINDOWS), POOL_HALO, POOL_GROUP), F32),
            pltpu.VMEM((POOL_HALO + tm, POOL_GROUP), F32),
            pltpu.VMEM((POOL_HALO + tm, POOL_GROUP), F32),
        ],
        compiler_params=pltpu.CompilerParams(
            dimension_semantics=("parallel", "arbitrary"),
            vmem_limit_bytes=VMEM_LIMIT_BYTES),
        name="pool_layer",
    )(x, w_in, w_grp, scale, w_out, ln_g, ln_b)


def _nsa_inproj_kernel(x_ref, wq_ref, wkv_ref, wz_ref, wgl_ref,
                       q_ref, cmp_ref, kx_ref, vx_ref, z_ref, gate_ref):
    tm = x_ref.shape[1]
    xb = x_ref[0].astype(BF16)
    q_ref[0] = (_dot(xb, wq_ref[...]) * (HEAD_DIM ** -0.5 * LOG2E)).astype(BF16)
    kv = _dot(xb, wkv_ref[...])
    for a in range(2):
        for g in range(N_KV):
            lo = a * D_KV + g * HEAD_DIM
            cmp_ref[0, a, g] = kv[:, lo:lo + HEAD_DIM]

    lane = lax.broadcasted_iota(jnp.int32, (tm, LANES), 1)
    ext = lane - HEAD_DIM
    low = ext < 0
    pos = pl.program_id(1) * tm + lax.broadcasted_iota(jnp.int32, (tm, LANES), 0)
    blk = _div_pow2(pos, SEL_BLOCK)
    alibi = jnp.where(ext < N_SLOPE_PIECES, blk, jnp.where(ext < 2 * N_SLOPE_PIECES, pos - blk * SEL_BLOCK, 0))
    k_win_ext = alibi.astype(F32)
    k_sel_ext = jnp.where(ext >= EXT_ONEHOT, jnp.where(ext - EXT_ONEHOT == blk, 1.0, 0.0), k_win_ext)
    ones_a = jnp.where(low, 1.0, 0.0).astype(BF16)
    ones_b = jnp.where(low, 0.0, 1.0).astype(BF16)
    for a in range(2, 6):
        for pair in range(N_KV // 2):
            lo = a * D_KV + pair * LANES
            both = kv[:, lo:lo + LANES]
            swapped = pltpu.roll(both, HEAD_DIM, axis=1)
            for g, (at_low, at_high) in ((2 * pair, (both, swapped)), (2 * pair + 1, (swapped, both))):
                if a % 2 == 0:
                    k_ext = k_sel_ext if a == 2 else k_win_ext
                    kx_ref[0, (a - 2) // 2, g] = jnp.where(low, at_low, k_ext).astype(BF16)
                else:
                    va = jnp.where(low, at_low, 0.0).astype(BF16)
                    vb = jnp.where(low, 0.0, at_high).astype(BF16)
                    vx_ref[0, a - 3, g] = jnp.concatenate([va, ones_a], axis=1)
                    vx_ref[0, a - 2, g] = jnp.concatenate([vb, ones_b], axis=1)

    z_ref[0] = _dot(xb, wz_ref[...])
    gate_ref[0] = jax.nn.sigmoid(_dot(xb, wgl_ref[...]))


def _nsa_inproj(x, wq, wkv, wz, wgl):
    B, S, D = x.shape
    tm = PROJ_TM
    return pl.pallas_call(
        _nsa_inproj_kernel,
        grid=(B, S // tm),
        in_specs=[
            pl.BlockSpec((1, tm, D), lambda b, s: (b, s, 0)),
            _const_spec(wq.shape),
            _const_spec(wkv.shape),
            _const_spec(wz.shape),
            _const_spec(wgl.shape),
        ],
        out_specs=[
            pl.BlockSpec((1, tm, D_ATT), lambda b, s: (b, s, 0)),
            pl.BlockSpec((1, 2, N_KV, tm, HEAD_DIM), lambda b, s: (b, 0, 0, s, 0)),
            pl.BlockSpec((1, 2, N_KV, tm, LANES), lambda b, s: (b, 0, 0, s, 0)),
            pl.BlockSpec((1, 4, N_KV, tm, 2 * LANES), lambda b, s: (b, 0, 0, s, 0)),
            pl.BlockSpec((1, tm, D_ATT), lambda b, s: (b, s, 0)),
            pl.BlockSpec((1, tm, GATE_LANES), lambda b, s: (b, s, 0)),
        ],
        out_shape=[
            jax.ShapeDtypeStruct((B, S, D_ATT), BF16),
            jax.ShapeDtypeStruct((B, 2, N_KV, S, HEAD_DIM), F32),
            jax.ShapeDtypeStruct((B, 2, N_KV, S, LANES), BF16),
            jax.ShapeDtypeStruct((B, 4, N_KV, S, 2 * LANES), BF16),
            jax.ShapeDtypeStruct((B, S, D_ATT), F32),
            jax.ShapeDtypeStruct((B, S, GATE_LANES), F32),
        ],
        compiler_params=pltpu.CompilerParams(
            dimension_semantics=("parallel", "parallel"),
            vmem_limit_bytes=VMEM_LIMIT_BYTES),
        name="nsa_inproj",
    )(x, wq, wkv, wz, wgl)


def _compress_kernel(c_ref, pos_ref, w1_ref, w2_ref, o_ref):
    n_grp, seq = c_ref.shape[2:4]
    n_row = seq // CMP_STRIDE
    width = CMP_STRIDE * HEAD_DIM
    groups = []
    for g in range(n_grp):
        pieces = [c_ref[0, 0, g, pl.ds(l, n_row, stride=CMP_STRIDE), :] for l in range(CMP_STRIDE)]
        groups.append(jnp.concatenate(pieces, axis=1))
    rows = jnp.concatenate(groups, axis=0)
    nxt = pltpu.roll(rows, shift=n_grp * n_row - 1, axis=0)
    first = (rows + pos_ref[0, 0:1, :]).astype(BF16)
    second = (nxt + pos_ref[0, 1:2, :]).astype(BF16)
    h = _dot(first, w1_ref[0, :width, :]) + _dot(second, w1_ref[0, width:, :])
    out = _dot(_silu(h).astype(BF16), w2_ref[0])
    o_ref[0, 0] = out.reshape(n_grp, n_row, HEAD_DIM).astype(BF16)


def _compress(cmp_in, pos, w1, w2):
    B = cmp_in.shape[0]
    S = cmp_in.shape[3]
    n_row = S // CMP_STRIDE
    width = CMP_STRIDE * HEAD_DIM
    return pl.pallas_call(
        _compress_kernel,
        grid=(2, B),
        in_specs=[
            pl.BlockSpec((1, 1, N_KV, S, HEAD_DIM), lambda a, b: (b, a, 0, 0, 0)),
            pl.BlockSpec((1, 2, width), lambda a, b: (a, 0, 0)),
            pl.BlockSpec((1,) + w1.shape[1:], lambda a, b: (a, 0, 0)),
            pl.BlockSpec((1,) + w2.shape[1:], lambda a, b: (a, 0, 0)),
        ],
        out_specs=pl.BlockSpec((1, 1, N_KV, n_row, HEAD_DIM), lambda a, b: (b, a, 0, 0, 0)),
        out_shape=jax.ShapeDtypeStruct((B, 2, N_KV, n_row, HEAD_DIM), BF16),
        compiler_params=pltpu.CompilerParams(
            dimension_semantics=("parallel", "parallel"),
            vmem_limit_bytes=VMEM_LIMIT_BYTES),
        name="nsa_compress",
    )(cmp_in, pos, w1, w2)


def _gate_expansion():
    r = jnp.arange(2 * GATE_LANES)[None, :, None] % GATE_LANES
    c = jnp.arange(N_BRANCH * HPG * HEAD_DIM)[None, None, :]
    g = jnp.arange(N_KV)[:, None, None]
    br, h = c // (HPG * HEAD_DIM), (c // HEAD_DIM) % HPG
    return (r == g * HPG * N_BRANCH + h * N_BRANCH + br).astype(BF16)


def _loop_chunks(tile, tq, kb):
    return (tile * tq) // kb


def _attn_pair_kernel(slope_ref, *refs):
    n_slot = 2 * ATT_PAIRS
    q_refs = refs[:n_slot]
    cmp_ref, kx_ref, vx_ref = refs[n_slot:n_slot + 3]
    g_refs = refs[n_slot + 3:2 * n_slot + 3]
    z_refs = refs[2 * n_slot + 3:3 * n_slot + 3]
    expand_ref = refs[3 * n_slot + 3]
    o_even_ref, o_odd_ref = refs[3 * n_slot + 4:3 * n_slot + 6]
    qs_ref, m_ref, acc_ref = refs[3 * n_slot + 6:]
    tq = q_refs[0].shape[1]
    n_cmp_pad = cmp_ref.shape[3]
    seq = kx_ref.shape[3]
    n_sel = seq // SEL_BLOCK
    n_tiles = seq // tq
    kb = ATT_KB
    n_pair = HPG // 2
    grp = pl.program_id(1)
    pstep = pl.program_id(2)
    assert kb == 2 * tq and kb - tq == WINDOW and EXT_LANES - EXT_ONEHOT == n_sel and n_tiles % 2 == 0
    n_static = _loop_chunks(0, tq, kb) + _loop_chunks(n_tiles - 1, tq, kb)
    assert all(_loop_chunks(2 * i, tq, kb) + _loop_chunks(n_tiles - 1 - 2 * i, tq, kb) == n_static
               for i in range(n_tiles // 2))

    lane = lax.broadcasted_iota(jnp.int32, (1, LANES), 1)
    low = lane < HEAD_DIM
    first_head = jnp.bitwise_and(lax.broadcasted_iota(jnp.int32, (1, 2 * LANES), 1), HEAD_DIM) == 0

    k_cmp = cmp_ref[0, 0, 0].astype(F32)
    v_cmp = cmp_ref[0, 1, 0].astype(F32)
    zero_half = jnp.zeros_like(v_cmp)
    k_cmp_x = jnp.concatenate([k_cmp, zero_half], axis=1).astype(BF16)
    v_cmp_x = (jnp.concatenate([v_cmp, zero_half], axis=1).astype(BF16),
               jnp.concatenate([zero_half, v_cmp], axis=1).astype(BF16))
    j_ov = lax.broadcasted_iota(jnp.int32, (n_sel, n_cmp_pad), 0) * SEL_BLOCK
    c_ov = lax.broadcasted_iota(jnp.int32, (n_sel, n_cmp_pad), 1) * CMP_STRIDE
    overlap_t = jnp.where(c_ov < j_ov + SEL_BLOCK, jnp.where(c_ov + CMP_BLOCK > j_ov, 1.0, 0.0), 0.0).astype(BF16)
    slope_rows, slopes = [], []
    for h in range(HPG):
        base = (grp * HPG + h) * SLOPE_ROW
        row = jnp.zeros((1, LANES), F32)
        for j in range(2 * N_SLOPE_PIECES):
            row = jnp.where(lane == HEAD_DIM + j, slope_ref[base + j], row)
        slope_rows.append(row)
        slopes.append(slope_ref[base + 2 * N_SLOPE_PIECES])
    c_row = lax.broadcasted_iota(jnp.int32, (1, n_cmp_pad), 1)
    j_blk = lax.broadcasted_iota(jnp.int32, (n_sel, tq), 0)
    j_loc = lax.broadcasted_iota(jnp.int32, (SUBLANES, tq), 0)
    n_tile = n_sel // SUBLANES
    top_n = float(min(SEL_TOPN, n_sel))

    def tile_lanes(x, n):
        return jnp.concatenate([x] * n, axis=1)

    def prepare(q_ref, q0):
        t_col = q0 + lax.broadcasted_iota(jnp.int32, (tq, 1), 0)
        qf = q_ref[0].astype(F32)
        q_src = []
        for pair in range(n_pair):
            both = qf[:, pair * LANES:(pair + 1) * LANES]
            q_src += [both, pltpu.roll(both, HEAD_DIM, axis=1)]
        q_win = [jnp.where(low, q_src[h], slope_rows[h]).astype(BF16) for h in range(HPG)]

        dist_c = t_col - (c_row * CMP_STRIDE + (CMP_BLOCK - 1))
        valid_c = dist_c >= 0
        dist_cf = dist_c.astype(F32)
        p_sum = jnp.zeros((tq, n_cmp_pad), F32)
        o_cmp = []
        for pair in range(n_pair):
            o_pair = None
            for hh in range(2):
                h = 2 * pair + hh
                s = _dot_nt(q_src[h].astype(BF16), k_cmp_x) - slopes[h] * dist_cf
                s = jnp.where(valid_c, s, NEG_INF)
                m = jnp.max(s, axis=-1, keepdims=True)
                p = jnp.where(valid_c, jnp.exp2(s - m), 0.0)
                l = jnp.sum(p, axis=-1, keepdims=True)
                p = p * jnp.where(l > 0.0, 1.0 / l, 0.0)
                p_sum = p_sum + p
                d = _dot(p.astype(BF16), v_cmp_x[hh])
                o_pair = d if o_pair is None else o_pair + d
            o_cmp.append(o_pair)

        p_hi = p_sum.astype(BF16)
        p_lo = (p_sum - p_hi.astype(F32)).astype(BF16)
        imp = _dot_nt(overlap_t, p_hi) + _dot_nt(overlap_t, p_lo)
        cur = _div_pow2(q0 + lax.broadcasted_iota(jnp.int32, (n_sel, tq), 1), SEL_BLOCK)
        forced = (j_blk == 0) | (j_blk == cur) | (j_blk == cur - 1)
        imp = jnp.where(forced, FORCE_SCORE, jnp.where(j_blk > cur, NEG_INF, imp))
        tiles = [imp[v * SUBLANES:(v + 1) * SUBLANES, :] for v in range(n_tile)]
        rank = [jnp.zeros((SUBLANES, tq), F32) for _ in range(n_tile)]
        for i in range(n_sel):
            v, i_loc = divmod(i, SUBLANES)
            row = tiles[v][i_loc:i_loc + 1, :]
            for w in range(n_tile):
                if w > v:
                    ahead = jnp.where(row >= tiles[w], 1.0, 0.0)
                elif w < v:
                    ahead = jnp.where(row > tiles[w], 1.0, 0.0)
                else:
                    ahead = jnp.where(j_loc > i_loc, jnp.where(row >= tiles[w], 1.0, 0.0),
                                      jnp.where(row > tiles[w], 1.0, 0.0))
                rank[w] = rank[w] + ahead
        mask_rows = [jnp.zeros((HEAD_DIM + EXT_ONEHOT, tq), F32)] + [jnp.where(r < top_n, 0.0, NEG_INF) for r in rank]
        mask_ext = jnp.transpose(jnp.concatenate(mask_rows, axis=0))
        q_aug = [jnp.where(low, q_src[h], mask_ext + slope_rows[h]).astype(BF16) for h in range(HPG)]
        return t_col, o_cmp, q_aug, q_win

    def chunk_init(queries, which_k, v_base, k0, n_keys, bias):
        k_c = kx_ref[0, which_k, 0, pl.ds(k0, n_keys), :]
        ms, accs = [], []
        for pair in range(n_pair):
            pv = None
            for hh in range(2):
                s = _dot_nt(queries[2 * pair + hh], k_c) + bias
                m = jnp.max(s, axis=-1, keepdims=True)
                p = jnp.exp2(s - m).astype(BF16)
                d = _dot(p, vx_ref[0, v_base + hh, 0, pl.ds(k0, n_keys), :])
                pv = d if pv is None else pv + d
                ms.append(m)
            accs.append(pv)
        return ms, accs

    tile_of = []
    for k in range(ATT_PAIRS):
        even_tile = 2 * (ATT_PAIRS * pstep + k)
        tile_of += [even_tile, n_tiles - 1 - even_tile]
    q0s = [tile_of[t] * tq for t in range(n_slot)]
    n_loop = [_div_pow2(q0, kb) for q0 in q0s]
    k0s = [pl.multiple_of(jnp.maximum(q0 - WINDOW, 0), LANES) for q0 in q0s]
    prepared = [prepare(q_refs[t], q0s[t]) for t in range(n_slot)]
    o_cmp = [p[1] for p in prepared]
    acc_win = []
    for t in range(n_slot):
        t_col, _, _, q_win = prepared[t]
        pos = k0s[t] + lax.broadcasted_iota(jnp.int32, (1, kb), 1)
        win_bias = jnp.where(pos <= t_col, jnp.where(pos > t_col - WINDOW, 0.0, NEG_INF), NEG_INF)
        acc_win.append(chunk_init(q_win, 1, 2, k0s[t], kb, win_bias)[1])
    for t in range(n_slot):
        t_col, _, q_aug, _ = prepared[t]
        if t % 2 == 0:
            s0, n_keys = pl.multiple_of(q0s[t], LANES), tq
        else:
            s0, n_keys = k0s[t], kb
        pos_s = s0 + lax.broadcasted_iota(jnp.int32, (1, n_keys), 1)
        sel_bias = jnp.where(pos_s <= t_col, 0.0, NEG_INF)
        m_sel, acc_sel = chunk_init(q_aug, 0, 0, s0, n_keys, sel_bias)
        for h in range(HPG):
            m_ref[t, h] = jnp.broadcast_to(m_sel[h], (tq, LANES))
            qs_ref[t, h] = q_aug[h]
        for pair in range(n_pair):
            acc_ref[t, pair] = acc_sel[pair]

    for k, j in [(k, j) for k in range(ATT_PAIRS) for j in range(n_static)]:
        mine = j < n_loop[2 * k]
        t = jnp.where(mine, 2 * k, 2 * k + 1)
        kc0 = pl.multiple_of(jnp.where(mine, j, j - n_loop[2 * k]) * kb, kb)
        k_c = kx_ref[0, 0, 0, pl.ds(kc0, kb), :]
        for pair in range(n_pair):
            pv, scale = None, []
            for hh in range(2):
                h = 2 * pair + hh
                s = _dot_nt(qs_ref[t, h], k_c)
                m_old = m_ref[t, h]
                m_new = jnp.maximum(m_old, jnp.max(s, axis=-1, keepdims=True))
                m_ref[t, h] = m_new
                scale.append(tile_lanes(jnp.exp2(m_old - m_new), 2))
                p = jnp.exp2(s - tile_lanes(m_new, kb // LANES)).astype(BF16)
                d = _dot(p, vx_ref[0, hh, 0, pl.ds(kc0, kb), :])
                pv = d if pv is None else pv + d
            acc_ref[t, pair] = jnp.where(first_head, scale[0], scale[1]) * acc_ref[t, pair] + pv

    width = HPG * HEAD_DIM
    for t, (g_ref, z_ref) in enumerate(zip(g_refs, z_refs)):
        gates = g_ref[0]
        g_hi = gates.astype(BF16)
        g_lo = (gates - g_hi.astype(F32)).astype(BF16)
        g_exp = _dot(jnp.concatenate([g_hi, g_lo], axis=1), expand_ref[0])
        outs = []
        for pair in range(n_pair):
            a_sel = acc_ref[t, pair]
            a_win = acc_win[t][pair]
            lo = pair * LANES
            outs.append(g_exp[:, lo:lo + LANES] * o_cmp[t][pair]
                        + g_exp[:, width + lo:width + lo + LANES] * (a_sel[:, :LANES] / a_sel[:, LANES:])
                        + g_exp[:, 2 * width + lo:2 * width + lo + LANES] * (a_win[:, :LANES] / a_win[:, LANES:]))
        gated = (jnp.concatenate(outs, axis=1) * _silu(z_ref[0])).astype(BF16)
        if t % 2 == 0:
            o_even_ref[0, pl.ds((t // 2) * tq, tq), :] = gated
        else:
            o_odd_ref[0, pl.ds((ATT_PAIRS - 1 - t // 2) * tq, tq), :] = gated


def _attention_pairs(slope_table, q, cmp_kv, kx, vx, gates, z):
    B, S, _ = q.shape
    tq = ATT_TQ
    n_tiles = S // tq
    n_half = n_tiles // 2
    n_cmp_pad = cmp_kv.shape[3]
    expand = _gate_expansion()
    width = HPG * HEAD_DIM

    n_slot = 2 * ATT_PAIRS
    n_step = n_half // ATT_PAIRS
    assert n_half == n_step * ATT_PAIRS

    def tile(slot, i):
        even_tile = 2 * (ATT_PAIRS * i + slot // 2)
        return even_tile if slot % 2 == 0 else n_tiles - 1 - even_tile

    def per_group(slot):
        return pl.BlockSpec((1, tq, width), lambda b, g, i, sl: (b, tile(slot, i), g))

    def all_groups(slot):
        return pl.BlockSpec((1, tq, GATE_LANES), lambda b, g, i, sl: (b, tile(slot, i), 0))

    grid_spec = pltpu.PrefetchScalarGridSpec(
        num_scalar_prefetch=1,
        grid=(B, N_KV, n_step),
        in_specs=[per_group(slot) for slot in range(n_slot)] + [
            pl.BlockSpec((1, 2, 1, n_cmp_pad, HEAD_DIM), lambda b, g, i, sl: (b, 0, g, 0, 0)),
            pl.BlockSpec((1, 2, 1, S, LANES), lambda b, g, i, sl: (b, 0, g, 0, 0)),
            pl.BlockSpec((1, 4, 1, S, 2 * LANES), lambda b, g, i, sl: (b, 0, g, 0, 0)),
        ] + [all_groups(slot) for slot in range(n_slot)] + [per_group(slot) for slot in range(n_slot)] + [
            pl.BlockSpec((1,) + expand.shape[1:], lambda b, g, i, sl: (g, 0, 0)),
        ],
        out_specs=[
            pl.BlockSpec((1, ATT_PAIRS * tq, width), lambda b, g, i, sl: (b, i, g)),
            pl.BlockSpec((1, ATT_PAIRS * tq, width), lambda b, g, i, sl: (b, n_step - 1 - i, g)),
        ],
        scratch_shapes=[
            pltpu.VMEM((n_slot, HPG, tq, LANES), BF16),
            pltpu.VMEM((n_slot, HPG, tq, LANES), F32),
            pltpu.VMEM((n_slot, HPG // 2, tq, 2 * LANES), F32),
        ],
    )
    half = jax.ShapeDtypeStruct((B, S // 2, D_ATT), BF16)
    return pl.pallas_call(
        _attn_pair_kernel,
        grid_spec=grid_spec,
        out_shape=[half, half],
        compiler_params=pltpu.CompilerParams(
            dimension_semantics=("parallel", "parallel", "arbitrary"),
            vmem_limit_bytes=VMEM_LIMIT_BYTES),
        name="nsa_attention",
    )(slope_table, *[q] * n_slot, cmp_kv, kx, vx, *[gates] * n_slot, *[z] * n_slot, expand)


def _nsa_out_kernel(o_even_ref, o_odd_ref, x_ref, w_ref, g_ref, b_ref, out_ref):
    n_tile, tq, d = o_even_ref.shape
    tiles = [ref[i] for i in range(n_tile) for ref in (o_even_ref, o_odd_ref)]
    for i, gated in enumerate(tiles):
        rows = pl.ds(i * tq, tq)
        y = _dot(gated, w_ref[...])
        out_ref[0, rows, :] = _layer_norm(DN_ALPHA * x_ref[0, rows, :] + y, g_ref[...], b_ref[...])


def _nsa_out(o_even, o_odd, x, w_out, ln_g, ln_b):
    B, S, D = x.shape
    tm, tq = PROJ_TM, ATT_TQ
    n_tile = tm // (2 * tq)
    assert tm == n_tile * 2 * tq
    tile = pl.BlockSpec((1, tm, D), lambda b, s: (b, s, 0))
    o_tile = pl.BlockSpec((n_tile, tq, D), lambda b, s: (b * (S // tm) + s, 0, 0))
    o_even = o_even.reshape(B * S // (2 * tq), tq, D)
    o_odd = o_odd.reshape(B * S // (2 * tq), tq, D)
    return pl.pallas_call(
        _nsa_out_kernel,
        grid=(B, S // tm),
        in_specs=[o_tile, o_tile, tile,
                  _const_spec(w_out.shape), _const_spec(ln_g.shape), _const_spec(ln_b.shape)],
        out_specs=tile,
        out_shape=jax.ShapeDtypeStruct((B, S, D), F32),
        compiler_params=pltpu.CompilerParams(
            dimension_semantics=("parallel", "parallel"),
            vmem_limit_bytes=VMEM_LIMIT_BYTES),
        name="nsa_out",
    )(o_even, o_odd, x, w_out, ln_g, ln_b)


def _slope_table():
    h = jnp.arange(1, N_HEADS + 1, dtype=F32)
    slope = (2.0 ** (-8.0 * h / N_HEADS)) * LOG2E
    pieces, rest = [], slope
    for _ in range(N_SLOPE_PIECES):
        piece = rest.astype(BF16).astype(F32)
        pieces.append(piece)
        rest = rest - piece
    cols = [SEL_BLOCK * p for p in pieces] + pieces + [slope]
    cols += [jnp.zeros_like(slope)] * (SLOPE_ROW - len(cols))
    return jnp.stack(cols, axis=1).reshape(-1)


def kernel(x, ln_g, ln_b, pool_w_in, pool_w_grp, pool_scale, pool_w_out, nsa_w_in, nsa_cmp_pos_k, nsa_cmp_w1_k,
           nsa_cmp_w2_k, nsa_cmp_pos_v, nsa_cmp_w1_v, nsa_cmp_w2_v, nsa_w_out):
    seq = x.shape[1]
    assert seq % max(POOL_TM, PROJ_TM, 2 * ATT_TQ) == 0 and x.shape[2] == D_MODEL
    assert seq >= WINDOW + ATT_TQ and WINDOW % ATT_TQ == 0 and ATT_TQ % SEL_BLOCK == 0
    assert seq // SEL_BLOCK == EXT_LANES - EXT_ONEHOT

    x = _pool_layer(x, pool_w_in[0].astype(BF16), pool_w_grp[0].astype(BF16), pool_scale[0][None, :],
                    pool_w_out[0].astype(BF16), ln_g[0][None, :], ln_b[0][None, :])

    w = nsa_w_in[0]
    kv_lo, z_lo, gl_lo = D_ATT, D_ATT + 6 * D_KV, 2 * D_ATT + 6 * D_KV
    wq = w[:, :kv_lo].astype(BF16)
    wkv = w[:, kv_lo:z_lo].astype(BF16)
    wz = w[:, z_lo:gl_lo].astype(BF16)
    wgl = jnp.pad(w[:, gl_lo:], ((0, 0), (0, GATE_LANES - N_BRANCH * N_HEADS))).astype(BF16)
    q, cmp_in, kx, vx, z, gates = _nsa_inproj(x, wq, wkv, wz, wgl)

    half = CMP_BLOCK // 2
    pos = jnp.stack([nsa_cmp_pos_k[0], nsa_cmp_pos_v[0]]).reshape(2, 2, half * HEAD_DIM)
    w1 = jnp.stack([nsa_cmp_w1_k[0], nsa_cmp_w1_v[0]]).astype(BF16)
    w2 = jnp.stack([nsa_cmp_w2_k[0], nsa_cmp_w2_v[0]]).astype(BF16)
    cmp_kv = _compress(cmp_in, pos, w1, w2)

    o_even, o_odd = _attention_pairs(_slope_table(), q, cmp_kv, kx, vx, gates, z)
    return _nsa_out(o_even, o_odd, x, nsa_w_out[0].astype(BF16), ln_g[1][None, :], ln_b[1][None, :])
```
